```python
import math
import functools
import jax
import jax.numpy as jnp
from jax import lax
import numpy as np

D_MODEL = 2048
BATCH = 4
SEQ = 2048
DEPTH = 2
DEC_BATCH = 128
DEC_SEQ = 1
PAST_LEN = 8192
PAGE_SIZE = 128

HEAD_DIM = 64
SSD_WIDTH = 768
SSD_HEADS = SSD_WIDTH // HEAD_DIM
SSD_GROUPS = 2
SSD_STATE = 128
SSD_CONV = 4
SSD_CHUNK = 128
SSD_XBC = SSD_WIDTH + 2 * SSD_GROUPS * SSD_STATE
SSD_IN = SSD_WIDTH + SSD_XBC + SSD_HEADS
MLA_HEADS = 5
MLA_NOPE = 128
MLA_ROPE = 64
MLA_V = 128
MLA_WIDTH = MLA_HEADS * MLA_V
MLA_Q_RANK = 512
MLA_KV_RANK = 512
MLA_IN = MLA_Q_RANK + MLA_KV_RANK + MLA_ROPE
MLA_SCALE = (MLA_NOPE + MLA_ROPE) ** -0.5
ROPE_THETA = 10000.0
ATTN_BLOCK = 128
RWKV_WIDTH = D_MODEL - SSD_WIDTH - MLA_WIDTH
RWKV_HEADS = RWKV_WIDTH // HEAD_DIM
RWKV_W_RANK = 64
RWKV_A_RANK = 64
RWKV_G_RANK = 128
RWKV_IN = 3 * RWKV_WIDTH + RWKV_W_RANK + RWKV_A_RANK + RWKV_G_RANK
RWKV_GN_EPS = 64e-5

TOTAL_IN = SSD_IN + MLA_IN + RWKV_IN
D_MIX = SSD_WIDTH + MLA_WIDTH + RWKV_WIDTH
FFN_DENSE = 5632
N_EXPERTS = 8
TOP_K = 2
FFN_EXPERT = 2816
N_DENSE_LAYERS = (DEPTH + 1) // 2
N_MOE_LAYERS = DEPTH // 2
ALPHA = (2.0 * DEPTH) ** 0.25
BETA = (8.0 * DEPTH) ** -0.25
LN_EPS = 1e-5
RMS_EPS = 1e-6

kernel_name = 'hybrid_ssd_mla_rwkv7_deepnorm_step'

F32 = jnp.float32


def layer_norm(x, g, b):
    xf = x.astype(F32)
    mu = jnp.mean(xf, -1, keepdims=True)
    var = jnp.mean(jnp.square(xf - mu), -1, keepdims=True)
    return ((xf - mu) * lax.rsqrt(var + LN_EPS) * g.astype(F32) + b.astype(F32)).astype(x.dtype)


def rms_norm(x, g):
    xf = x.astype(F32)
    return (xf * lax.rsqrt(jnp.mean(jnp.square(xf), -1, keepdims=True) + RMS_EPS) * g.astype(F32)).astype(x.dtype)


def rope(x, pos):
    half = MLA_ROPE // 2
    inv = ROPE_THETA ** (-jnp.arange(half, dtype=F32) / half)
    ang = pos.astype(F32)[:, None] * inv[None, :]
    cos = jnp.cos(ang)[:, None, :]
    sin = jnp.sin(ang)[:, None, :]
    xf = x.astype(F32)
    x1, x2 = xf[..., :half], xf[..., half:]
    return jnp.concatenate([x1 * cos - x2 * sin, x2 * cos + x1 * sin], -1).astype(x.dtype)


def causal_conv(u, buf, w, b):
    seqlen = u.shape[1]
    full = jnp.concatenate([buf.astype(u.dtype), u], axis=1)
    out = b
    for k in range(SSD_CONV):
        out = out + full[:, k:k + seqlen] * w[k]
    return out, full[:, seqlen:]


def ssd_scan(xs, dt, a, bm, cm, h0):
    bsz, seqlen, nh, hp = xs.shape
    rep = nh // SSD_GROUPS
    q = min(SSD_CHUNK, seqlen)
    nc = -(-seqlen // q)
    pad = nc * q - seqlen

    def chunks(t):
        t = jnp.pad(t, [(0, 0), (0, pad)] + [(0, 0)] * (t.ndim - 2))
        return t.reshape((bsz, nc, q) + t.shape[2:])

    xc = chunks(xs.astype(F32))
    dtc = chunks(dt)
    bc = chunks(jnp.repeat(bm, rep, axis=2).astype(F32))
    cc = chunks(jnp.repeat(cm, rep, axis=2).astype(F32))
    acum = jnp.cumsum(jnp.moveaxis(dtc * a, 2, 3), axis=-1)
    seg = acum[..., :, None] - acum[..., None, :]
    causal = jnp.tril(jnp.ones((q, q), dtype=bool))
    lmat = jnp.exp(jnp.where(causal, seg, -jnp.inf))
    scores = jnp.einsum('bcihn,bcjhn->bchij', cc, bc) * lmat
    y_diag = jnp.einsum('bchij,bcjh,bcjhp->bcihp', scores, dtc, xc)
    decay_end = jnp.exp(acum[..., -1:] - acum)
    chunk_state = jnp.einsum('bchj,bcjh,bcjhn,bcjhp->bchpn', decay_end, dtc, bc, xc)
    chunk_decay = jnp.exp(acum[..., -1])

    def step(h, inp):
        d_a, s = inp
        return d_a[..., None, None] * h + s, h

    h_final, h_start = lax.scan(step, h0.astype(F32),
                                (jnp.moveaxis(chunk_decay, 1, 0), jnp.moveaxis(chunk_state, 1, 0)))
    h_start = jnp.moveaxis(h_start, 0, 1)
    y_off = jnp.einsum('bcihn,bchpn,bchi->bcihp', cc, h_start, jnp.exp(acum))
    y = (y_diag + y_off).reshape(bsz, nc * q, nh, hp)[:, :seqlen]
    return y, h_final


def ssd_mixer(u, conv_buf, h0, p):
    bsz, seqlen, _ = u.shape
    z = u[..., :SSD_WIDTH]
    xbc = u[..., SSD_WIDTH:SSD_WIDTH + SSD_XBC]
    dt_raw = u[..., SSD_WIDTH + SSD_XBC:]
    xbc, new_buf = causal_conv(xbc, conv_buf, p['ssd_conv_w'], p['ssd_conv_b'])
    xbc = jax.nn.silu(xbc)
    gn = SSD_GROUPS * SSD_STATE
    xs = xbc[..., :SSD_WIDTH].reshape(bsz, seqlen, SSD_HEADS, HEAD_DIM)
    bm = xbc[..., SSD_WIDTH:SSD_WIDTH + gn].reshape(bsz, seqlen, SSD_GROUPS, SSD_STATE)
    cm = xbc[..., SSD_WIDTH + gn:].reshape(bsz, seqlen, SSD_GROUPS, SSD_STATE)
    dt = jax.nn.softplus(dt_raw.astype(F32) + p['ssd_dt_bias'].astype(F32))
    a = -jnp.exp(p['ssd_a_log'].astype(F32))
    y, h = ssd_scan(xs, dt, a, bm, cm, h0)
    y = y + p['ssd_d'].astype(F32)[:, None] * xs.astype(F32)
    y = y.reshape(bsz, seqlen, SSD_WIDTH) * jax.nn.silu(z.astype(F32))
    y = rms_norm(y.reshape(bsz, seqlen, SSD_GROUPS, SSD_WIDTH // SSD_GROUPS),
                 p['ssd_norm'].reshape(SSD_GROUPS, SSD_WIDTH // SSD_GROUPS))
    return y.reshape(bsz, seqlen, SSD_WIDTH).astype(u.dtype), new_buf, h.astype(h0.dtype)


def mla_project(u, pos, p):
    c_q = rms_norm(u[..., :MLA_Q_RANK], p['mla_q_norm'])
    c_kv = rms_norm(u[..., MLA_Q_RANK:MLA_Q_RANK + MLA_KV_RANK], p['mla_kv_norm'])
    k_pe = rope(u[..., MLA_Q_RANK + MLA_KV_RANK:][:, :, None, :], pos)[:, :, 0]
    q = jnp.einsum('blr,rhe->blhe', c_q, p['mla_q_up'])
    q_pe = rope(q[..., MLA_NOPE:], pos)
    q_lat = jnp.einsum('blhn,rhn->blhr', q[..., :MLA_NOPE], p['mla_kv_up'][..., :MLA_NOPE])
    return q_lat, q_pe, c_kv, k_pe


def mla_attend(q_lat, q_pe, segments):
    scores = []
    for c, pe, mask in segments:
        s = (jnp.einsum('bqhr,bkr->bhqk', q_lat, c)
             + jnp.einsum('bqhp,bkp->bhqk', q_pe, pe)).astype(F32) * MLA_SCALE
        if mask is not None:
            s = jnp.where(mask, s, -jnp.inf)
        scores.append(s)
    probs = jax.nn.softmax(jnp.concatenate(scores, axis=-1), axis=-1)
    out, start = 0.0, 0
    for c, _, _ in segments:
        n = c.shape[1]
        out = out + jnp.einsum('bhqk,bkr->bqhr', probs[..., start:start + n].astype(c.dtype), c)
        start += n
    return out


def mla_prompt_attend(q_lat, q_pe, c_kv, k_pe):
    bsz, seqlen, nh, r = q_lat.shape
    blk = min(ATTN_BLOCK, seqlen)
    nblk = seqlen // blk
    k_pos = jnp.arange(seqlen)

    def one_block(i):
        start = i * blk
        ql = lax.dynamic_slice_in_dim(q_lat, start, blk, axis=1)
        qp = lax.dynamic_slice_in_dim(q_pe, start, blk, axis=1)
        mask = k_pos[None, :] <= (start + jnp.arange(blk))[:, None]
        return mla_attend(ql, qp, [(c_kv, k_pe, mask)])

    out = lax.map(one_block, jnp.arange(nblk))
    return jnp.moveaxis(out, 0, 1).reshape(bsz, seqlen, nh, r)


def mla_sample_attend(q_lat, q_pe, c_kv, k_pe, pool_ckv, pool_kpe, page_table):
    dbsz, dseq = q_lat.shape[:2]
    c_past = pool_ckv[page_table].reshape(dbsz, -1, MLA_KV_RANK)
    pe_past = pool_kpe[page_table].reshape(dbsz, -1, MLA_ROPE)
    mask_new = jnp.tril(jnp.ones((dseq, dseq), dtype=bool))
    return mla_attend(q_lat, q_pe, [(c_past, pe_past, None), (c_kv, k_pe, mask_new)])


def rwkv_mixer(u, shift_prev, s0, p):
    bsz, seqlen, _ = u.shape
    prev = jnp.concatenate([shift_prev[:, None, :].astype(u.dtype), u[:, :-1]], axis=1)
    ux = u + (prev - u) * p['rw_mu']
    o1 = RWKV_WIDTH
    o2 = 2 * RWKV_WIDTH
    o3 = 3 * RWKV_WIDTH
    o4 = o3 + RWKV_W_RANK
    o5 = o4 + RWKV_A_RANK
    r, k, v = ux[..., :o1], ux[..., o1:o2], ux[..., o2:o3]
    xw, xa, xg = ux[..., o3:o4].astype(F32), ux[..., o4:o5].astype(F32), ux[..., o5:].astype(F32)
    w = -jax.nn.softplus(-(p['rw_w0'].astype(F32) + jnp.tanh(xw) @ p['rw_w_up'].astype(F32))) - 0.5
    decay = jnp.exp(-jnp.exp(w))
    a = jax.nn.sigmoid(p['rw_a0'].astype(F32) + xa @ p['rw_a_up'].astype(F32))
    g = jax.nn.sigmoid(xg) @ p['rw_g_up'].astype(F32)

    def heads(t):
        return t.astype(F32).reshape(bsz, seqlen, RWKV_HEADS, HEAD_DIM)

    r, k, v, decay, a = heads(r), heads(k), heads(v), heads(decay), heads(a)
    kk = k * p['rw_k_k'].astype(F32).reshape(RWKV_HEADS, HEAD_DIM)
    kk = kk / jnp.maximum(jnp.sqrt(jnp.sum(jnp.square(kk), -1, keepdims=True)), 1e-12)
    k = k * (1.0 + (a - 1.0) * p['rw_k_a'].astype(F32).reshape(RWKV_HEADS, HEAD_DIM))

    def step(s, inp):
        r_t, w_t, k_t, v_t, kk_t, a_t = inp
        sa = jnp.einsum('bhij,bhj->bhi', s, -kk_t)
        s = (s * w_t[:, :, None, :] + sa[..., None] * (kk_t * a_t)[:, :, None, :]
             + v_t[..., None] * k_t[:, :, None, :])
        return s, jnp.einsum('bhij,bhj->bhi', s, r_t)

    seq_first = lambda t: jnp.moveaxis(t, 1, 0)
    s_final, y = lax.scan(step, s0.astype(F32),
                          (seq_first(r), seq_first(decay), seq_first(k), seq_first(v), seq_first(kk), seq_first(a)))
    y = jnp.moveaxis(y, 0, 1)
    mu = jnp.mean(y, -1, keepdims=True)
    var = jnp.mean(jnp.square(y - mu), -1, keepdims=True)
    y = ((y - mu) * lax.rsqrt(var + RWKV_GN_EPS)).reshape(bsz, seqlen, RWKV_WIDTH)
    y = y * p['rw_ln_g'].astype(F32) + p['rw_ln_b'].astype(F32)
    bonus = jnp.sum(r * k * p['rw_r_k'].astype(F32), -1, keepdims=True) * v
    y = (y + bonus.reshape(bsz, seqlen, RWKV_WIDTH)) * g
    return y.astype(u.dtype), u[:, -1], s_final.astype(s0.dtype)


def swiglu(x, w_gu, w_down):
    f = w_down.shape[0]
    h = x @ w_gu
    return (jax.nn.silu(h[..., :f]) * h[..., f:]) @ w_down


def moe_ffn(x, router, w_gu, w_down):
    probs = jax.nn.softmax((x @ router).astype(F32), axis=-1)
    top_p, top_i = lax.top_k(probs, TOP_K)
    top_p = top_p / jnp.sum(top_p, -1, keepdims=True)
    gates = jnp.sum(jax.nn.one_hot(top_i, N_EXPERTS, dtype=F32) * top_p[..., None], axis=-2)
    y = jnp.zeros(x.shape, F32)
    for e in range(N_EXPERTS):
        y = y + gates[..., e:e + 1] * swiglu(x, w_gu[e], w_down[e]).astype(F32)
    return y.astype(x.dtype)


def trunk_layer(x, pos, conv_buf, ssm_h, wkv_s, shift_prev, attend, p, use_moe):
    bsz, seqlen, _ = x.shape
    u = x @ p['w_in']
    u_ssd = u[..., :SSD_IN]
    u_mla = u[..., SSD_IN:SSD_IN + MLA_IN]
    u_rw = u[..., SSD_IN + MLA_IN:]
    y_ssd, conv_new, h_new = ssd_mixer(u_ssd, conv_buf, ssm_h, p)
    q_lat, q_pe, c_kv, k_pe = mla_project(u_mla, pos, p)
    o_lat = attend(q_lat, q_pe, c_kv, k_pe)
    y_mla = jnp.einsum('blhr,rhv->blhv', o_lat, p['mla_kv_up'][..., MLA_NOPE:]).reshape(bsz, seqlen, MLA_WIDTH)
    y_rw, shift_new, s_new = rwkv_mixer(u_rw, shift_prev, wkv_s, p)
    mix = jnp.concatenate([y_ssd, y_mla.astype(x.dtype), y_rw], axis=-1) @ p['w_out']
    x = layer_norm(ALPHA * x + mix, p['ln1_g'], p['ln1_b'])
    if use_moe:
        f = moe_ffn(x, p['moe_router'], p['moe_gu'], p['moe_down'])
    else:
        f = swiglu(x, p['ffn_gu'], p['ffn_down'])
    x = layer_norm(ALPHA * x + f, p['ln2_g'], p['ln2_b'])
    return x, (c_kv, k_pe, h_new, conv_new, s_new, shift_new)


def setup_inputs(seed: int = 0) -> dict:
    key = jax.random.key(seed)
    keys = jax.random.split(key, 64)
    counter = [0]

    def nk():
        counter[0] += 1
        return keys[counter[0] - 1]

    def normal(shape, scale):
        return jax.random.normal(nk(), shape, F32) * scale

    def uniform(shape, lo, hi):
        return jax.random.uniform(nk(), shape, F32, lo, hi)

    n_pages = PAST_LEN // PAGE_SIZE
    n_pool = (DEC_BATCH * n_pages * 5) // 4
    page_table = jax.random.permutation(nk(), n_pool)[:DEC_BATCH * n_pages].reshape(DEC_BATCH, n_pages).astype(jnp.int32)
    dt0 = jnp.exp(uniform((DEPTH, SSD_HEADS), math.log(1e-3), math.log(1e-1)))
    return {
        'x_prompt': normal((BATCH, SEQ, D_MODEL), 1.0),
        'x_sample': normal((DEC_BATCH, DEC_SEQ, D_MODEL), 1.0),
        'cache_ckv': normal((DEPTH, n_pool, PAGE_SIZE, MLA_KV_RANK), 1.0),
        'cache_kpe': normal((DEPTH, n_pool, PAGE_SIZE, MLA_ROPE), 1.0),
        'page_table': page_table,
        'state_ssm': normal((DEPTH, DEC_BATCH, SSD_HEADS, HEAD_DIM, SSD_STATE), 0.1),
        'state_conv': normal((DEPTH, DEC_BATCH, SSD_CONV - 1, SSD_XBC), 1.0),
        'state_wkv': normal((DEPTH, DEC_BATCH, RWKV_HEADS, HEAD_DIM, HEAD_DIM), 0.3),
        'state_shift': normal((DEPTH, DEC_BATCH, RWKV_IN), 1.0),
        'w_in': normal((DEPTH, D_MODEL, TOTAL_IN), D_MODEL ** -0.5),
        'w_out': normal((DEPTH, D_MIX, D_MODEL), BETA * D_MIX ** -0.5),
        'ln1_g': 1.0 + normal((DEPTH, D_MODEL), 0.02),
        'ln1_b': normal((DEPTH, D_MODEL), 0.02),
        'ln2_g': 1.0 + normal((DEPTH, D_MODEL), 0.02),
        'ln2_b': normal((DEPTH, D_MODEL), 0.02),
        'ssd_conv_w': normal((DEPTH, SSD_CONV, SSD_XBC), SSD_CONV ** -0.5),
        'ssd_conv_b': normal((DEPTH, SSD_XBC), 0.02),
        'ssd_dt_bias': dt0 + jnp.log(-jnp.expm1(-dt0)),
        'ssd_a_log': jnp.log(uniform((DEPTH, SSD_HEADS), 1.0, 16.0)),
        'ssd_d': 1.0 + normal((DEPTH, SSD_HEADS), 0.1),
        'ssd_norm': 1.0 + normal((DEPTH, SSD_WIDTH), 0.02),
        'mla_q_norm': 1.0 + normal((DEPTH, MLA_Q_RANK), 0.02),
        'mla_kv_norm': 1.0 + normal((DEPTH, MLA_KV_RANK), 0.02),
        'mla_q_up': normal((DEPTH, MLA_Q_RANK, MLA_HEADS, MLA_NOPE + MLA_ROPE), MLA_Q_RANK ** -0.5),
        'mla_kv_up': normal((DEPTH, MLA_KV_RANK, MLA_HEADS, MLA_NOPE + MLA_V), MLA_KV_RANK ** -0.5),
        'rw_mu': uniform((DEPTH, RWKV_IN), 0.0, 1.0),
        'rw_w0': uniform((DEPTH, RWKV_WIDTH), -6.0, 1.0),
        'rw_w_up': normal((DEPTH, RWKV_W_RANK, RWKV_WIDTH), 0.1),
        'rw_a0': normal((DEPTH, RWKV_WIDTH), 0.1),
        'rw_a_up': normal((DEPTH, RWKV_A_RANK, RWKV_WIDTH), 0.5 * RWKV_A_RANK ** -0.5),
        'rw_g_up': normal((DEPTH, RWKV_G_RANK, RWKV_WIDTH), RWKV_G_RANK ** -0.5),
        'rw_k_k': 0.85 + normal((DEPTH, RWKV_WIDTH), 0.05),
        'rw_k_a': 1.0 + normal((DEPTH, RWKV_WIDTH), 0.05),
        'rw_r_k': normal((DEPTH, RWKV_HEADS, HEAD_DIM), 0.1),
        'rw_ln_g': 1.0 + normal((DEPTH, RWKV_WIDTH), 0.02),
        'rw_ln_b': normal((DEPTH, RWKV_WIDTH), 0.02),
        'ffn_gu': normal((N_DENSE_LAYERS, D_MODEL, 2 * FFN_DENSE), D_MODEL ** -0.5),
        'ffn_down': normal((N_DENSE_LAYERS, FFN_DENSE, D_MODEL), BETA * FFN_DENSE ** -0.5),
        'moe_router': normal((N_MOE_LAYERS, D_MODEL, N_EXPERTS), D_MODEL ** -0.5),
        'moe_gu': normal((N_MOE_LAYERS, N_EXPERTS, D_MODEL, 2 * FFN_EXPERT), D_MODEL ** -0.5),
        'moe_down': normal((N_MOE_LAYERS, N_EXPERTS, FFN_EXPERT, D_MODEL), BETA * FFN_EXPERT ** -0.5),
    }


def reference(x_prompt, x_sample, cache_ckv, cache_kpe, page_table, state_ssm, state_conv, state_wkv, state_shift,
              w_in, w_out, ln1_g, ln1_b, ln2_g, ln2_b,
              ssd_conv_w, ssd_conv_b, ssd_dt_bias, ssd_a_log, ssd_d, ssd_norm,
              mla_q_norm, mla_kv_norm, mla_q_up, mla_kv_up,
              rw_mu, rw_w0, rw_w_up, rw_a0, rw_a_up, rw_g_up, rw_k_k, rw_k_a, rw_r_k, rw_ln_g, rw_ln_b,
              ffn_gu, ffn_down, moe_router, moe_gu, moe_down):
    bsz, seqlen, _ = x_prompt.shape
    dseq = x_sample.shape[1]
    past_len = page_table.shape[1] * PAGE_SIZE
    pos_prompt = jnp.arange(seqlen)
    pos_sample = past_len + jnp.arange(dseq)
    dt = x_prompt.dtype
    yp, ys = x_prompt, x_sample
    outs_p = [[] for _ in range(6)]
    outs_s = [[] for _ in range(6)]
    for l in range(DEPTH):
        use_moe = (l % 2 == 1)
        p = dict(w_in=w_in[l], w_out=w_out[l], ln1_g=ln1_g[l], ln1_b=ln1_b[l], ln2_g=ln2_g[l], ln2_b=ln2_b[l],
                 ssd_conv_w=ssd_conv_w[l], ssd_conv_b=ssd_conv_b[l], ssd_dt_bias=ssd_dt_bias[l],
                 ssd_a_log=ssd_a_log[l], ssd_d=ssd_d[l], ssd_norm=ssd_norm[l],
                 mla_q_norm=mla_q_norm[l], mla_kv_norm=mla_kv_norm[l], mla_q_up=mla_q_up[l], mla_kv_up=mla_kv_up[l],
                 rw_mu=rw_mu[l], rw_w0=rw_w0[l], rw_w_up=rw_w_up[l], rw_a0=rw_a0[l], rw_a_up=rw_a_up[l],
                 rw_g_up=rw_g_up[l], rw_k_k=rw_k_k[l], rw_k_a=rw_k_a[l], rw_r_k=rw_r_k[l],
                 rw_ln_g=rw_ln_g[l], rw_ln_b=rw_ln_b[l])
        if use_moe:
            p['moe_router'] = moe_router[l // 2]
            p['moe_gu'] = moe_gu[l // 2]
            p['moe_down'] = moe_down[l // 2]
        else:
            p['ffn_gu'] = ffn_gu[l // 2]
            p['ffn_down'] = ffn_down[l // 2]
        yp, st_p = trunk_layer(yp, pos_prompt,
                               jnp.zeros((bsz, SSD_CONV - 1, SSD_XBC), dt),
                               jnp.zeros((bsz, SSD_HEADS, HEAD_DIM, SSD_STATE), dt),
                               jnp.zeros((bsz, RWKV_HEADS, HEAD_DIM, HEAD_DIM), dt),
                               jnp.zeros((bsz, RWKV_IN), dt),
                               mla_prompt_attend, p, use_moe)
        attend_s = functools.partial(mla_sample_attend, pool_ckv=cache_ckv[l], pool_kpe=cache_kpe[l],
                                     page_table=page_table)
        ys, st_s = trunk_layer(ys, pos_sample, state_conv[l], state_ssm[l], state_wkv[l], state_shift[l],
                               attend_s, p, use_moe)
        for i in range(6):
            outs_p[i].append(st_p[i])
            outs_s[i].append(st_s[i])
    ckv_p, kpe_p, ssm_p, conv_p, wkv_p, shift_p = [jnp.stack(o) for o in outs_p]
    ckv_s, kpe_s, ssm_s, conv_s, wkv_s, shift_s = [jnp.stack(o) for o in outs_s]
    return (yp, ys, ckv_p, kpe_p, ckv_s, kpe_s, ssm_p, ssm_s, conv_p, conv_s, wkv_p, wkv_s, shift_p, shift_s)
```

```python
import functools
import math

import jax
import jax.numpy as jnp
from jax import lax
from jax.experimental import pallas as pl
from jax.experimental.pallas import tpu as pltpu

F32 = jnp.float32
BF16 = jnp.bfloat16
HI = lax.Precision.HIGHEST

D_MODEL = 2048
PAGE_SIZE = 128
HEAD_DIM = 64
SSD_WIDTH = 768
SSD_HEADS = 12
SSD_GROUPS = 2
SSD_STATE = 128
SSD_CONV = 4
SSD_CHUNK = 128
SSD_XBC = 1280
SSD_IN = 2060
SSD_GW = SSD_WIDTH // SSD_GROUPS
SSD_UP = 2176
MLA_HEADS = 5
MLA_NOPE = 128
MLA_ROPE = 64
MLA_V = 128
MLA_WIDTH = 640
MLA_Q_RANK = 512
MLA_KV_RANK = 512
MLA_IN = 1088
MLA_UP = 1152
MLA_QK = 640
MLA_SCALE = (MLA_NOPE + MLA_ROPE) ** -0.5
ROPE_THETA = 10000.0
RWKV_WIDTH = 640
RWKV_HEADS = 10
RWKV_IN = 2176
RWKV_GN_EPS = 64e-5
RWKV_CHUNK = 64
RWKV_SUB = 16
FFN_DENSE = 5632
N_EXPERTS = 8
FFN_EXPERT = 2816
DEPTH = 2
ALPHA = (2.0 * DEPTH) ** 0.25
LN_EPS = 1e-5
RMS_EPS = 1e-6
LANES = 128
NEG_BIG = -1e30

VMEM_LIMIT = 56 * 1024 * 1024


def _cp(*sem):
    return pltpu.CompilerParams(dimension_semantics=sem, vmem_limit_bytes=VMEM_LIMIT)


def _dot(a, b):
    return jnp.dot(a.astype(BF16), b.astype(BF16), preferred_element_type=F32)


def _dot_nt(a, b):
    return lax.dot_general(a.astype(BF16), b.astype(BF16), (((1,), (1,)), ((), ())), preferred_element_type=F32)


def _dot_tn(a, b):
    return lax.dot_general(a.astype(BF16), b.astype(BF16), (((0,), (0,)), ((), ())), preferred_element_type=F32)


def _dot_hi(a, b):
    return jnp.dot(a, b, precision=HI, preferred_element_type=F32)


def _dot_nt_hi(a, b):
    return lax.dot_general(a, b, (((1,), (1,)), ((), ())), precision=HI, preferred_element_type=F32)


def _dot_tn_hi(a, b):
    return lax.dot_general(a, b, (((0,), (0,)), ((), ())), precision=HI, preferred_element_type=F32)


def _sigmoid(x):
    return 1.0 / (1.0 + jnp.exp(-x))


def _silu(x):
    return x * _sigmoid(x)


def _softplus(x):
    return jnp.maximum(x, 0.0) + jnp.log1p(jnp.exp(-jnp.abs(x)))


def _iota(shape, dim):
    return lax.broadcasted_iota(jnp.int32, shape, dim)


def _seg_sum(x, width, seg):
    same = (_iota((width, width), 0) // seg) == (_iota((width, width), 1) // seg)
    ones = jnp.where(same, 1.0, 0.0).astype(BF16)
    hi = x.astype(BF16)
    lo = (x - hi.astype(F32)).astype(BF16)
    return (jnp.dot(hi, ones, preferred_element_type=F32) + jnp.dot(lo, ones, preferred_element_type=F32))


def _layer_norm(x, g, b):
    mu = jnp.mean(x, -1, keepdims=True)
    xc = x - mu
    var = jnp.mean(xc * xc, -1, keepdims=True)
    return xc * lax.rsqrt(var + LN_EPS) * g + b


def _rms_norm(x, g):
    return x * lax.rsqrt(jnp.mean(x * x, -1, keepdims=True) + RMS_EPS) * g


def _mm_kernel(x_ref, w_ref, o_ref):
    o_ref[...] = jnp.dot(x_ref[...].astype(BF16), w_ref[...], preferred_element_type=F32).astype(o_ref.dtype)


def _matmul(x, w, tm, out_dtype=F32):
    m, k = x.shape
    n = w.shape[1]
    return pl.pallas_call(
        _mm_kernel,
        grid=(m // tm,),
        in_specs=[pl.BlockSpec((tm, k), lambda i: (i, 0)), pl.BlockSpec((k, n), lambda i: (0, 0))],
        out_specs=pl.BlockSpec((tm, n), lambda i: (i, 0)),
        out_shape=jax.ShapeDtypeStruct((m, n), out_dtype),
        compiler_params=_cp("parallel"),
        name="matmul",
    )(x, w)


def _mm_res_ln_kernel(a_ref, w_ref, res_ref, g_ref, b_ref, o_ref):
    mix = jnp.dot(a_ref[...], w_ref[...], preferred_element_type=F32)
    o_ref[...] = _layer_norm(ALPHA * res_ref[...] + mix, g_ref[...], b_ref[...])


def _matmul_res_ln(a, w, res, g, b, tm):
    m, k = a.shape
    n = w.shape[1]
    return pl.pallas_call(
        _mm_res_ln_kernel,
        grid=(m // tm,),
        in_specs=[pl.BlockSpec((tm, k), lambda i: (i, 0)), pl.BlockSpec((k, n), lambda i: (0, 0)),
                  pl.BlockSpec((tm, n), lambda i: (i, 0)), pl.BlockSpec((1, n), lambda i: (0, 0)),
                  pl.BlockSpec((1, n), lambda i: (0, 0))],
        out_specs=pl.BlockSpec((tm, n), lambda i: (i, 0)),
        out_shape=jax.ShapeDtypeStruct((m, n), F32),
        compiler_params=_cp("parallel"),
        name="out_proj_ln",
    )(a, w, res, g, b)


def _ffn_kernel(x_ref, wg_ref, wu_ref, wd_ref, g_ref, b_ref, o_ref, xb_scr):
    f = pl.program_id(1)

    @pl.when(f == 0)
    def _():
        xb_scr[...] = x_ref[...].astype(BF16)
        o_ref[...] = ALPHA * x_ref[...]

    xb = xb_scr[...]
    hg = jnp.dot(xb, wg_ref[...], preferred_element_type=F32)
    hu = jnp.dot(xb, wu_ref[...], preferred_element_type=F32)
    act = (_silu(hg) * hu).astype(BF16)
    o_ref[...] += jnp.dot(act, wd_ref[...], preferred_element_type=F32)

    @pl.when(f == pl.num_programs(1) - 1)
    def _():
        o_ref[...] = _layer_norm(o_ref[...], g_ref[...], b_ref[...])


def _ffn_dense(x, w_gu, w_down, g, b, tm, tf):
    m, d = x.shape
    fdim = w_down.shape[0]
    nf = fdim // tf
    return pl.pallas_call(
        _ffn_kernel,
        grid=(m // tm, nf),
        in_specs=[pl.BlockSpec((tm, d), lambda i, f: (i, 0)),
                  pl.BlockSpec((d, tf), lambda i, f: (0, f)),
                  pl.BlockSpec((d, tf), lambda i, f: (0, nf + f)),
                  pl.BlockSpec((tf, d), lambda i, f: (f, 0)),
                  pl.BlockSpec((1, d), lambda i, f: (0, 0)),
                  pl.BlockSpec((1, d), lambda i, f: (0, 0))],
        out_specs=pl.BlockSpec((tm, d), lambda i, f: (i, 0)),
        out_shape=jax.ShapeDtypeStruct((m, d), F32),
        scratch_shapes=[pltpu.VMEM((tm, d), BF16)],
        compiler_params=_cp("parallel", "arbitrary"),
        name="ffn_dense",
    )(x, w_gu, w_gu, w_down, g, b)


def _router_kernel(x_ref, wr_ref, gates_ref):
    logits = _dot_hi(x_ref[...], wr_ref[...])
    lane = _iota(logits.shape, 1)
    valid = lane < N_EXPERTS
    logits = jnp.where(valid, logits, NEG_BIG)
    mx = jnp.max(logits, -1, keepdims=True)
    ex = jnp.where(valid, jnp.exp(logits - mx), 0.0)
    probs = ex / jnp.sum(ex, -1, keepdims=True)
    p1 = jnp.max(probs, -1, keepdims=True)
    i1 = jnp.min(jnp.where(probs == p1, lane, LANES), -1, keepdims=True)
    first = lane == i1
    rest = jnp.where(first | (~valid), -1.0, probs)
    p2 = jnp.max(rest, -1, keepdims=True)
    i2 = jnp.min(jnp.where(rest == p2, lane, LANES), -1, keepdims=True)
    second = lane == i2
    tot = p1 + p2
    gates_ref[...] = jnp.where(first, p1 / tot, 0.0) + jnp.where(second, p2 / tot, 0.0)


def _router(x, wr_pad, tm):
    m, d = x.shape
    return pl.pallas_call(
        _router_kernel,
        grid=(m // tm,),
        in_specs=[pl.BlockSpec((tm, d), lambda i: (i, 0)), pl.BlockSpec((d, LANES), lambda i: (0, 0))],
        out_specs=pl.BlockSpec((tm, LANES), lambda i: (i, 0)),
        out_shape=jax.ShapeDtypeStruct((m, LANES), F32),
        compiler_params=_cp("parallel"),
        name="moe_router",
    )(x, wr_pad)


def _moe_kernel(x_ref, gates_ref, wg_ref, wu_ref, wd_ref, g_ref, b_ref, o_ref, xb_scr):
    e = pl.program_id(1)
    f = pl.program_id(2)

    @pl.when((e == 0) & (f == 0))
    def _():
        xb_scr[...] = x_ref[...].astype(BF16)
        o_ref[...] = ALPHA * x_ref[...]

    gates = gates_ref[...]
    gate = jnp.sum(jnp.where(_iota(gates.shape, 1) == e, gates, 0.0), -1, keepdims=True)
    xb = xb_scr[...]
    hg = jnp.dot(xb, wg_ref[...], preferred_element_type=F32)
    hu = jnp.dot(xb, wu_ref[...], preferred_element_type=F32)
    act = (_silu(hg) * hu * gate).astype(BF16)
    o_ref[...] += jnp.dot(act, wd_ref[...], preferred_element_type=F32)

    @pl.when((e == pl.num_programs(1) - 1) & (f == pl.num_programs(2) - 1))
    def _():
        o_ref[...] = _layer_norm(o_ref[...], g_ref[...], b_ref[...])


def _moe_dense(x, gates, w_gu, w_down, g, b, tm, tf):
    m, d = x.shape
    ne, fdim, _ = w_down.shape
    nf = fdim // tf
    return pl.pallas_call(
        _moe_kernel,
        grid=(m // tm, ne, nf),
        in_specs=[pl.BlockSpec((tm, d), lambda i, e, f: (i, 0)),
                  pl.BlockSpec((tm, LANES), lambda i, e, f: (i, 0)),
                  pl.BlockSpec((None, d, tf), lambda i, e, f: (e, 0, f)),
                  pl.BlockSpec((None, d, tf), lambda i, e, f: (e, 0, nf + f)),
                  pl.BlockSpec((None, tf, d), lambda i, e, f: (e, f, 0)),
                  pl.BlockSpec((1, d), lambda i, e, f: (0, 0)),
                  pl.BlockSpec((1, d), lambda i, e, f: (0, 0))],
        out_specs=pl.BlockSpec((tm, d), lambda i, e, f: (i, 0)),
        out_shape=jax.ShapeDtypeStruct((m, d), F32),
        scratch_shapes=[pltpu.VMEM((tm, d), BF16)],
        compiler_params=_cp("parallel", "arbitrary", "arbitrary"),
        name="moe_dense",
    )(x, gates, w_gu, w_gu, w_down, g, b)


def _ssd_gate_norm(y, xs, z, d_rep, norm_w):
    y = (y + d_rep * xs) * _silu(z)
    outs = []
    for g in range(SSD_GROUPS):
        sl = slice(g * SSD_GW, (g + 1) * SSD_GW)
        outs.append(_rms_norm(y[:, sl], norm_w[:, sl]))
    return jnp.concatenate(outs, axis=-1)


def _ssd_prompt_kernel(u_ref, cw_ref, cb_ref, dtb_ref, alog_ref, drep_ref, nw_ref,
                       y_ref, hout_ref, xf_scr, h_scr, y_scr):
    q = SSD_CHUNK
    c = pl.program_id(1)

    @pl.when(c == 0)
    def _():
        xf_scr[0:8, :] = jnp.zeros((8, SSD_XBC), F32)
        h_scr[...] = jnp.zeros_like(h_scr)

    xf_scr[8:8 + q, :] = u_ref[:, SSD_WIDTH:SSD_WIDTH + SSD_XBC]
    acc = cb_ref[...] + xf_scr[5:5 + q, :] * cw_ref[0:1, :]
    for k in range(1, SSD_CONV):
        acc = acc + xf_scr[5 + k:5 + k + q, :] * cw_ref[k:k + 1, :]
    xf_scr[5:8, :] = xf_scr[5 + q:8 + q, :]
    xbc = _silu(acc)
    xs = xbc[:, :SSD_WIDTH]
    bmat = xbc[:, SSD_WIDTH:SSD_WIDTH + SSD_GROUPS * SSD_STATE]
    cmat = xbc[:, SSD_WIDTH + SSD_GROUPS * SSD_STATE:]

    dt = _softplus(u_ref[:, SSD_WIDTH + SSD_XBC:] + dtb_ref[...])
    da = dt * (-jnp.exp(alog_ref[...]))
    row = _iota((q, q), 0)
    col = _iota((q, q), 1)
    causal = row >= col
    acum = _dot_hi(jnp.where(causal, 1.0, 0.0), da)
    dt_t = dt.T
    acum_t = _dot_hi(da.T, jnp.where(row <= col, 1.0, 0.0))

    scores = [_dot_nt(cmat[:, g * SSD_STATE:(g + 1) * SSD_STATE], bmat[:, g * SSD_STATE:(g + 1) * SSD_STATE])
              for g in range(SSD_GROUPS)]
    rep = SSD_HEADS // SSD_GROUPS
    for h in range(SSD_HEADS):
        g = h // rep
        hs = slice(h * HEAD_DIM, (h + 1) * HEAD_DIM)
        gs = slice(g * SSD_STATE, (g + 1) * SSD_STATE)
        a_col = acum[:, h:h + 1]
        a_last = acum[q - 1:q, h:h + 1]
        seg = a_col - acum_t[h:h + 1, :]
        lmat = jnp.exp(jnp.where(causal, seg, NEG_BIG))
        m = scores[g] * lmat * dt_t[h:h + 1, :]
        xs_h = xs[:, hs]
        h_prev = h_scr[h]
        y_scr[:, hs] = _dot(m, xs_h) + jnp.exp(a_col) * _dot_nt(cmat[:, gs], h_prev)
        wj = jnp.exp(a_last - a_col) * dt[:, h:h + 1]
        h_scr[h] = jnp.exp(a_last) * h_prev + _dot_tn(xs_h * wj, bmat[:, gs])

    y_ref[...] = _ssd_gate_norm(y_scr[...], xs, u_ref[:, :SSD_WIDTH], drep_ref[...], nw_ref[...]).astype(y_ref.dtype)

    @pl.when(c == pl.num_programs(1) - 1)
    def _():
        hout_ref[...] = h_scr[...]


def _ssd_prompt(u_ssd, bsz, seqlen, cw, cb, dtb, alog, drep, nw):
    q = SSD_CHUNK
    nc = seqlen // q
    full = lambda shape: pl.BlockSpec(shape, lambda b, c: (0,) * len(shape))
    return pl.pallas_call(
        _ssd_prompt_kernel,
        grid=(bsz, nc),
        in_specs=[pl.BlockSpec((q, SSD_UP), lambda b, c: (b * nc + c, 0)),
                  full((SSD_CONV, SSD_XBC)), full((1, SSD_XBC)), full((1, LANES)), full((1, LANES)),
                  full((1, SSD_WIDTH)), full((1, SSD_WIDTH))],
        out_specs=[pl.BlockSpec((q, SSD_WIDTH), lambda b, c: (b * nc + c, 0)),
                   pl.BlockSpec((None, SSD_HEADS, HEAD_DIM, SSD_STATE), lambda b, c: (b, 0, 0, 0))],
        out_shape=[jax.ShapeDtypeStruct((bsz * seqlen, SSD_WIDTH), BF16),
                   jax.ShapeDtypeStruct((bsz, SSD_HEADS, HEAD_DIM, SSD_STATE), F32)],
        scratch_shapes=[pltpu.VMEM((8 + q, SSD_XBC), F32),
                        pltpu.VMEM((SSD_HEADS, HEAD_DIM, SSD_STATE), F32),
                        pltpu.VMEM((q, SSD_WIDTH), F32)],
        compiler_params=_cp("parallel", "arbitrary"),
        name="ssd_prompt",
    )(u_ssd, cw, cb, dtb, alog, drep, nw)


def _ssd_sample_kernel(u_ref, cbuf_ref, h0_ref, cw_ref, cb_ref, dtb_ref, alog_ref, arep_ref, drep_ref, nw_ref,
                       y_ref, hout_ref, y_scr):
    bt = u_ref.shape[0]
    x_new = u_ref[:, SSD_WIDTH:SSD_WIDTH + SSD_XBC]
    acc = cb_ref[...] + x_new * cw_ref[SSD_CONV - 1:SSD_CONV, :]
    for k in range(SSD_CONV - 1):
        acc = acc + cbuf_ref[:, k * SSD_XBC:(k + 1) * SSD_XBC] * cw_ref[k:k + 1, :]
    xbc = _silu(acc)
    xs = xbc[:, :SSD_WIDTH]
    bmat = xbc[:, SSD_WIDTH:SSD_WIDTH + SSD_GROUPS * SSD_STATE]
    cmat = xbc[:, SSD_WIDTH + SSD_GROUPS * SSD_STATE:]
    dt = _softplus(u_ref[:, SSD_WIDTH + SSD_XBC:] + dtb_ref[...])
    expand = jnp.where(_iota((LANES, SSD_WIDTH), 0) == _iota((LANES, SSD_WIDTH), 1) // HEAD_DIM, 1.0, 0.0)
    dt_rep = _dot_hi(dt, expand)
    dec_rep = jnp.exp(dt_rep * arep_ref[...])
    dtx = dt_rep * xs
    rows = _iota((bt, 1), 0)
    ones = jnp.ones((bt, SSD_STATE), F32)
    y_scr[...] = jnp.zeros_like(y_scr)
    rep = SSD_HEADS // SSD_GROUPS
    for bi in range(bt):
        sel = rows == bi
        dtx_b = jnp.where(sel, dtx, 0.0)
        dec_b = jnp.where(sel, dec_rep, 0.0)
        for g in range(SSD_GROUPS):
            cs = slice(g * SSD_GW, (g + 1) * SSD_GW)
            gs = slice(g * SSD_STATE, (g + 1) * SSD_STATE)
            h0 = h0_ref[bi, g * rep:(g + 1) * rep].reshape(SSD_GW, SSD_STATE)
            outer = _dot_tn_hi(dtx_b[:, cs], bmat[:, gs])
            decm = _dot_tn_hi(dec_b[:, cs], ones)
            hn = decm * h0 + outer
            hout_ref[bi, g * rep:(g + 1) * rep] = hn.reshape(rep, HEAD_DIM, SSD_STATE)
            y_scr[:, cs] += _dot_nt_hi(jnp.where(sel, cmat[:, gs], 0.0), hn)
    y_ref[...] = _ssd_gate_norm(y_scr[...], xs, u_ref[:, :SSD_WIDTH], drep_ref[...], nw_ref[...]).astype(y_ref.dtype)


def _ssd_sample(u_ssd, row0, dbsz, conv_flat, h0, cw, cb, dtb, alog, arep, drep, nw, bt=8):
    full = lambda shape: pl.BlockSpec(shape, lambda i: (0,) * len(shape))
    blk0 = row0 // bt
    return pl.pallas_call(
        _ssd_sample_kernel,
        grid=(dbsz // bt,),
        in_specs=[pl.BlockSpec((bt, SSD_UP), lambda i: (blk0 + i, 0)),
                  pl.BlockSpec((bt, (SSD_CONV - 1) * SSD_XBC), lambda i: (i, 0)),
                  pl.BlockSpec((bt, SSD_HEADS, HEAD_DIM, SSD_STATE), lambda i: (i, 0, 0, 0)),
                  full((SSD_CONV, SSD_XBC)), full((1, SSD_XBC)), full((1, LANES)), full((1, LANES)),
                  full((1, SSD_WIDTH)), full((1, SSD_WIDTH)), full((1, SSD_WIDTH))],
        out_specs=[pl.BlockSpec((bt, SSD_WIDTH), lambda i: (i, 0)),
                   pl.BlockSpec((bt, SSD_HEADS, HEAD_DIM, SSD_STATE), lambda i: (i, 0, 0, 0))],
        out_shape=[jax.ShapeDtypeStruct((dbsz, SSD_WIDTH), BF16),
                   jax.ShapeDtypeStruct((dbsz, SSD_HEADS, HEAD_DIM, SSD_STATE), F32)],
        scratch_shapes=[pltpu.VMEM((bt, SSD_WIDTH), F32)],
        compiler_params=_cp("parallel"),
        name="ssd_sample",
    )(u_ssd, conv_flat, h0, cw, cb, dtb, alog, arep, drep, nw)


def _rw_prep_body(u, prev, mu, w0, w_up, a0, a_up, g_up, k_k, k_a):
    o1, o2, o3 = RWKV_WIDTH, 2 * RWKV_WIDTH, 3 * RWKV_WIDTH
    o4, o5 = o3 + 64, o3 + 128
    ux = u + (prev - u) * mu
    r, k, v = ux[:, :o1], ux[:, o1:o2], ux[:, o2:o3]
    xw, xa, xg = ux[:, o3:o4], ux[:, o4:o5], ux[:, o5:]
    w = -_softplus(-(w0 + _dot(jnp.tanh(xw), w_up))) - 0.5
    lw = -jnp.exp(w)
    a = _sigmoid(a0 + _dot(xa, a_up))
    g = _dot(_sigmoid(xg), g_up)
    kk = k * k_k
    nrm = jnp.maximum(jnp.sqrt(_seg_sum(kk * kk, RWKV_WIDTH, HEAD_DIM)), 1e-12)
    kap = kk / nrm
    k = k * (1.0 + (a - 1.0) * k_a)
    return r, lw, k, v, kap, kap * a, g


def _rw_prep_prompt_kernel(tiles_per_seq, u_ref, mu_ref, w0_ref, wup_ref, a0_ref, aup_ref, gup_ref, kk_ref, ka_ref,
                           r_ref, lw_ref, k_ref, v_ref, kap_ref, beta_ref, g_ref, scr):
    tm = u_ref.shape[0]
    i = pl.program_id(0)

    @pl.when(i % tiles_per_seq == 0)
    def _():
        scr[0:8, :] = jnp.zeros((8, RWKV_IN), F32)

    scr[8:8 + tm, :] = u_ref[...]
    prev = scr[7:7 + tm, :]
    outs = _rw_prep_body(u_ref[...], prev, mu_ref[...], w0_ref[...], wup_ref[...], a0_ref[...], aup_ref[...],
                         gup_ref[...], kk_ref[...], ka_ref[...])
    scr[7:8, :] = scr[7 + tm:8 + tm, :]
    for ref, val in zip((r_ref, lw_ref, k_ref, v_ref, kap_ref, beta_ref, g_ref), outs):
        ref[...] = val


def _rw_prep_sample_kernel(u_ref, prev_ref, mu_ref, w0_ref, wup_ref, a0_ref, aup_ref, gup_ref, kk_ref, ka_ref,
                           r_ref, lw_ref, k_ref, v_ref, kap_ref, beta_ref, g_ref):
    outs = _rw_prep_body(u_ref[...], prev_ref[...], mu_ref[...], w0_ref[...], wup_ref[...], a0_ref[...],
                         aup_ref[...], gup_ref[...], kk_ref[...], ka_ref[...])
    for ref, val in zip((r_ref, lw_ref, k_ref, v_ref, kap_ref, beta_ref, g_ref), outs):
        ref[...] = val


def _rw_param_specs(nargs):
    full = lambda shape: pl.BlockSpec(shape, lambda *a: (0,) * len(shape))
    del nargs
    return [full((1, RWKV_IN)), full((1, RWKV_WIDTH)), full((64, RWKV_WIDTH)), full((1, RWKV_WIDTH)),
            full((64, RWKV_WIDTH)), full((128, RWKV_WIDTH)), full((1, RWKV_WIDTH)), full((1, RWKV_WIDTH))]


def _rw_prep_prompt(u_rw, nrows, seqlen, params, tm=256):
    outs = [jax.ShapeDtypeStruct((nrows, RWKV_WIDTH), F32)] * 7
    return pl.pallas_call(
        functools.partial(_rw_prep_prompt_kernel, seqlen // tm),
        grid=(nrows // tm,),
        in_specs=[pl.BlockSpec((tm, RWKV_IN), lambda i: (i, 0))] + _rw_param_specs(1),
        out_specs=[pl.BlockSpec((tm, RWKV_WIDTH), lambda i: (i, 0))] * 7,
        out_shape=outs,
        scratch_shapes=[pltpu.VMEM((8 + tm, RWKV_IN), F32)],
        compiler_params=_cp("arbitrary"),
        name="rwkv_prep_prompt",
    )(u_rw, *params)


def _rw_prep_sample(u_rw, row0, dbsz, prev, params):
    outs = [jax.ShapeDtypeStruct((dbsz, RWKV_WIDTH), F32)] * 7
    return pl.pallas_call(
        _rw_prep_sample_kernel,
        grid=(1,),
        in_specs=[pl.BlockSpec((dbsz, RWKV_IN), lambda i: (row0 // dbsz, 0)),
                  pl.BlockSpec((dbsz, RWKV_IN), lambda i: (0, 0))] + _rw_param_specs(1),
        out_specs=[pl.BlockSpec((dbsz, RWKV_WIDTH), lambda i: (0, 0))] * 7,
        out_shape=outs,
        compiler_params=_cp("arbitrary"),
        name="rwkv_prep_sample",
    )(u_rw, prev, *params)


def _rw_post(y, r, k, v, g, rk, ln_g, ln_b):
    mu = _seg_sum(y, RWKV_WIDTH, HEAD_DIM) * (1.0 / HEAD_DIM)
    yc = y - mu
    var = _seg_sum(yc * yc, RWKV_WIDTH, HEAD_DIM) * (1.0 / HEAD_DIM)
    yn = yc * lax.rsqrt(var + RWKV_GN_EPS) * ln_g + ln_b
    bonus = _seg_sum(r * k * rk, RWKV_WIDTH, HEAD_DIM) * v
    return (yn + bonus) * g


def _unit_lower_inverse(a, c, sub):
    row = _iota((c, c), 0)
    col = _iota((c, c), 1)
    eye = jnp.where(row == col, 1.0, 0.0)
    d = jnp.where((row // sub) == (col // sub), a, 0.0)
    n = a - d
    x = eye - d
    p = d
    steps = int(math.log2(sub))
    for _ in range(steps - 1):
        p = _dot_hi(p, p)
        x = x + _dot_hi(x, p)
    e = _dot_hi(x, n)
    y = eye - e
    p = e
    for _ in range(int(math.log2(c // sub)) - 1):
        p = _dot_hi(p, p)
        y = y + _dot_hi(y, p)
    return _dot_hi(y, x)


def _rw_scan_kernel(r_ref, lw_ref, k_ref, v_ref, kap_ref, beta_ref, g_ref, rk_ref, lng_ref, lnb_ref,
                    y_ref, sout_ref, s_scr, y_scr):
    c = RWKV_CHUNK
    ci = pl.program_id(1)

    @pl.when(ci == 0)
    def _():
        s_scr[...] = jnp.zeros_like(s_scr)

    row = _iota((c, c), 0)
    col = _iota((c, c), 1)
    lower = row >= col
    strict = row > col
    r, lw, k, v = r_ref[...], lw_ref[...], k_ref[...], v_ref[...]
    logp = _dot_hi(jnp.where(lower, 1.0, 0.0), lw)
    p_inc = jnp.exp(logp)
    p_inv = jnp.exp(-logp)
    p_end = jnp.exp(logp[c - 1:c, :])
    rt = r * p_inc
    kt = kap_ref[...] * jnp.exp(logp - lw)
    kb = k * p_inv
    bb = beta_ref[...] * p_inv
    khat = kb * p_end
    bhat = bb * p_end
    for h in range(RWKV_HEADS):
        hs = slice(h * HEAD_DIM, (h + 1) * HEAD_DIM)
        a_m = jnp.where(strict, _dot_nt_hi(kt[:, hs], bb[:, hs]), 0.0)
        b_m = jnp.where(strict, _dot_nt_hi(kt[:, hs], kb[:, hs]), 0.0)
        m1 = jnp.where(lower, _dot_nt_hi(rt[:, hs], kb[:, hs]), 0.0)
        m2 = jnp.where(lower, _dot_nt_hi(rt[:, hs], bb[:, hs]), 0.0)
        tinv = _unit_lower_inverse(a_m, c, RWKV_SUB)
        s0 = s_scr[h]
        v_h = v[:, hs]
        u_m = _dot_hi(tinv, _dot_nt_hi(kt[:, hs], s0) + _dot_hi(b_m, v_h))
        y_scr[:, hs] = _dot_nt_hi(rt[:, hs], s0) + _dot_hi(m1, v_h) - _dot_hi(m2, u_m)
        s_scr[h] = s0 * p_end[:, hs] + _dot_tn_hi(v_h, khat[:, hs]) - _dot_tn_hi(u_m, bhat[:, hs])

    y_ref[...] = _rw_post(y_scr[...], r, k, v, g_ref[...], rk_ref[...], lng_ref[...], lnb_ref[...]).astype(y_ref.dtype)

    @pl.when(ci == pl.num_programs(1) - 1)
    def _():
        sout_ref[...] = s_scr[...]


def _rw_scan_prompt(seqs, bsz, seqlen, rk, ln_g, ln_b):
    c = RWKV_CHUNK
    nc = seqlen // c
    tok = pl.BlockSpec((c, RWKV_WIDTH), lambda b, i: (b * nc + i, 0))
    par = pl.BlockSpec((1, RWKV_WIDTH), lambda b, i: (0, 0))
    return pl.pallas_call(
        _rw_scan_kernel,
        grid=(bsz, nc),
        in_specs=[tok] * 7 + [par] * 3,
        out_specs=[tok, pl.BlockSpec((None, RWKV_HEADS, HEAD_DIM, HEAD_DIM), lambda b, i: (b, 0, 0, 0))],
        out_shape=[jax.ShapeDtypeStruct((bsz * seqlen, RWKV_WIDTH), BF16),
                   jax.ShapeDtypeStruct((bsz, RWKV_HEADS, HEAD_DIM, HEAD_DIM), F32)],
        scratch_shapes=[pltpu.VMEM((RWKV_HEADS, HEAD_DIM, HEAD_DIM), F32), pltpu.VMEM((c, RWKV_WIDTH), F32)],
        compiler_params=_cp("parallel", "arbitrary"),
        name="rwkv_scan_prompt",
    )(*seqs, rk, ln_g, ln_b)


def _rw_step_kernel(r_ref, lw_ref, k_ref, v_ref, kap_ref, beta_ref, g_ref, s0_ref, rk_ref, lng_ref, lnb_ref,
                    y_ref, sout_ref, y_scr):
    bt = r_ref.shape[0]
    rows = _iota((bt, 1), 0)
    r, k, v = r_ref[...], k_ref[...], v_ref[...]
    dec = jnp.exp(lw_ref[...])
    y_scr[...] = jnp.zeros_like(y_scr)
    for bi in range(bt):
        sel = rows == bi
        r_b = jnp.where(sel, r, 0.0)
        v_b = jnp.where(sel, v, 0.0)
        for h in range(RWKV_HEADS):
            hs = slice(h * HEAD_DIM, (h + 1) * HEAD_DIM)
            s0 = s0_ref[bi, h]
            sa = jnp.sum(s0 * (-kap_ref[bi:bi + 1, hs]), -1, keepdims=True)
            sn = s0 * dec[bi:bi + 1, hs] + sa * beta_ref[bi:bi + 1, hs] + _dot_tn_hi(v_b[:, hs], k[:, hs])
            sout_ref[bi, h] = sn
            y_scr[:, hs] += _dot_nt_hi(r_b[:, hs], sn)
    y_ref[...] = _rw_post(y_scr[...], r, k, v, g_ref[...], rk_ref[...], lng_ref[...], lnb_ref[...]).astype(y_ref.dtype)


def _rw_step_sample(seqs, s0, dbsz, rk, ln_g, ln_b, bt=8):
    tok = pl.BlockSpec((bt, RWKV_WIDTH), lambda i: (i, 0))
    par = pl.BlockSpec((1, RWKV_WIDTH), lambda i: (0, 0))
    st = pl.BlockSpec((bt, RWKV_HEADS, HEAD_DIM, HEAD_DIM), lambda i: (i, 0, 0, 0))
    return pl.pallas_call(
        _rw_step_kernel,
        grid=(dbsz // bt,),
        in_specs=[tok] * 7 + [st] + [par] * 3,
        out_specs=[tok, st],
        out_shape=[jax.ShapeDtypeStruct((dbsz, RWKV_WIDTH), BF16),
                   jax.ShapeDtypeStruct((dbsz, RWKV_HEADS, HEAD_DIM, HEAD_DIM), F32)],
        scratch_shapes=[pltpu.VMEM((bt, RWKV_WIDTH), F32)],
        compiler_params=_cp("parallel"),
        name="rwkv_step_sample",
    )(*seqs, s0, rk, ln_g, ln_b)


def _mla_proj_kernel(u_ref, cos_ref, sin_ref, qn_ref, kvn_ref, wqn_ref, wqr_ref, wqx_ref, wkn_ref,
                     ckv_ref, kpe_ref, q_ref, kcat_ref):
    tm = u_ref.shape[0]
    c_q = _rms_norm(u_ref[:, :MLA_Q_RANK], qn_ref[...])
    c_kv = _rms_norm(u_ref[:, MLA_Q_RANK:MLA_Q_RANK + MLA_KV_RANK], kvn_ref[...])
    cos = cos_ref[...]
    sin = sin_ref[...]
    o = MLA_Q_RANK + MLA_KV_RANK
    k_pe = u_ref[:, o:o + MLA_ROPE] * cos[:, :MLA_ROPE] + u_ref[:, o + MLA_ROPE:o + 2 * MLA_ROPE] * sin[:, :MLA_ROPE]
    ckv_ref[...] = c_kv
    kpe_ref[...] = k_pe
    zpad = jnp.zeros((tm, MLA_QK - MLA_KV_RANK - MLA_ROPE), F32)
    kcat_ref[...] = jnp.concatenate([c_kv, k_pe, zpad], axis=-1).astype(BF16)
    cqb = c_q.astype(BF16)
    q_nope = jnp.dot(cqb, wqn_ref[...], preferred_element_type=F32)
    q_pe = (jnp.dot(cqb, wqr_ref[...], preferred_element_type=F32) * cos
            + jnp.dot(cqb, wqx_ref[...], preferred_element_type=F32) * sin)
    for h in range(MLA_HEADS):
        q_lat = _dot(q_nope[:, h * MLA_NOPE:(h + 1) * MLA_NOPE], wkn_ref[h])
        qh = jnp.concatenate([q_lat, q_pe[:, h * MLA_ROPE:(h + 1) * MLA_ROPE], zpad], axis=-1) * MLA_SCALE
        q_ref[h] = qh.astype(BF16)


def _mla_proj(u_mla, cos5, sin5, qn, kvn, wqn, wqr, wqx, wkn, tm):
    t = u_mla.shape[0]
    full = lambda shape: pl.BlockSpec(shape, lambda i: (0,) * len(shape))
    hr = MLA_HEADS * MLA_ROPE
    return pl.pallas_call(
        _mla_proj_kernel,
        grid=(t // tm,),
        in_specs=[pl.BlockSpec((tm, MLA_UP), lambda i: (i, 0)),
                  pl.BlockSpec((tm, hr), lambda i: (i, 0)), pl.BlockSpec((tm, hr), lambda i: (i, 0)),
                  full((1, MLA_Q_RANK)), full((1, MLA_KV_RANK)),
                  full((MLA_Q_RANK, MLA_HEADS * MLA_NOPE)), full((MLA_Q_RANK, hr)), full((MLA_Q_RANK, hr)),
                  full((MLA_HEADS, MLA_NOPE, MLA_KV_RANK))],
        out_specs=[pl.BlockSpec((tm, MLA_KV_RANK), lambda i: (i, 0)),
                   pl.BlockSpec((tm, MLA_ROPE), lambda i: (i, 0)),
                   pl.BlockSpec((MLA_HEADS, tm, MLA_QK), lambda i: (0, i, 0)),
                   pl.BlockSpec((tm, MLA_QK), lambda i: (i, 0))],
        out_shape=[jax.ShapeDtypeStruct((t, MLA_KV_RANK), F32), jax.ShapeDtypeStruct((t, MLA_ROPE), F32),
                   jax.ShapeDtypeStruct((MLA_HEADS, t, MLA_QK), BF16), jax.ShapeDtypeStruct((t, MLA_QK), BF16)],
        compiler_params=_cp("parallel"),
        name="mla_proj",
    )(u_mla, cos5, sin5, qn, kvn, wqn, wqr, wqx, wkn)


def _mla_prompt_kernel(tq, tk, q_ref, k_ref, wv_ref, y_ref, m_scr, l_scr, acc_scr):
    i = pl.program_id(1)
    j = pl.program_id(2)
    rows = MLA_HEADS * tq
    last = (i * tq + tq - 1) // tk

    @pl.when(j == 0)
    def _():
        m_scr[...] = jnp.full(m_scr.shape, NEG_BIG, F32)
        l_scr[...] = jnp.zeros_like(l_scr)
        acc_scr[...] = jnp.zeros_like(acc_scr)

    @pl.when(j <= last)
    def _():
        q = q_ref[...].reshape(rows, MLA_QK)
        kc = k_ref[...]
        s = lax.dot_general(q, kc, (((1,), (1,)), ((), ())), preferred_element_type=F32)
        qpos = i * tq + _iota((rows, tk), 0) % tq
        kpos = j * tk + _iota((rows, tk), 1)
        s = jnp.where(kpos <= qpos, s, NEG_BIG)
        m_old = m_scr[...]
        m_new = jnp.maximum(m_old, jnp.max(s, -1, keepdims=True))
        alpha = jnp.exp(m_old - m_new)
        p = jnp.exp(s - m_new)
        l_scr[...] = alpha * l_scr[...] + jnp.sum(p, -1, keepdims=True)
        acc_scr[...] = alpha * acc_scr[...] + jnp.dot(p.astype(BF16), kc[:, :MLA_KV_RANK],
                                                      preferred_element_type=F32)
        m_scr[...] = m_new

    @pl.when(j == pl.num_programs(2) - 1)
    def _():
        o = (acc_scr[...] / l_scr[...]).reshape(MLA_HEADS, tq, MLA_KV_RANK)
        ys = [_dot(o[h], wv_ref[h]) for h in range(MLA_HEADS)]
        y_ref[...] = jnp.concatenate(ys, axis=-1).astype(y_ref.dtype)


def _mla_prompt_attend(q, kcat, wv, bsz, seqlen, tq=256, tk=512):
    nq = seqlen // tq
    nk = seqlen // tk
    rows = MLA_HEADS * tq

    def k_map(b, i, j):
        return (b * nk + jnp.minimum(j, (i * tq + tq - 1) // tk), 0)

    return pl.pallas_call(
        functools.partial(_mla_prompt_kernel, tq, tk),
        grid=(bsz, nq, nk),
        in_specs=[pl.BlockSpec((MLA_HEADS, tq, MLA_QK), lambda b, i, j: (0, b * nq + i, 0)),
                  pl.BlockSpec((tk, MLA_QK), k_map),
                  pl.BlockSpec((MLA_HEADS, MLA_KV_RANK, MLA_V), lambda b, i, j: (0, 0, 0))],
        out_specs=pl.BlockSpec((tq, MLA_WIDTH), lambda b, i, j: (b * nq + i, 0)),
        out_shape=jax.ShapeDtypeStruct((bsz * seqlen, MLA_WIDTH), BF16),
        scratch_shapes=[pltpu.VMEM((rows, 1), F32), pltpu.VMEM((rows, 1), F32),
                        pltpu.VMEM((rows, MLA_KV_RANK), F32)],
        compiler_params=_cp("parallel", "parallel", "arbitrary"),
        name="mla_prompt_attend",
    )(q, kcat, wv)


def _mla_sample_kernel(npg, pt_ref, q_ref, cnew_ref, pnew_ref, *refs):
    del pt_ref
    ckv_refs = refs[:npg]
    kpe_refs = refs[npg:2 * npg]
    o_ref = refs[2 * npg]
    m_scr, l_scr, acc_scr = refs[2 * npg + 1:]
    s_idx = pl.program_id(1)

    @pl.when(s_idx == 0)
    def _():
        m_scr[...] = jnp.full(m_scr.shape, NEG_BIG, F32)
        l_scr[...] = jnp.zeros_like(l_scr)
        acc_scr[...] = jnp.zeros_like(acc_scr)

    q = q_ref[...]
    q_lat = q[:, :MLA_KV_RANK]
    q_pe = q[:, MLA_KV_RANK:MLA_KV_RANK + MLA_ROPE]
    cs = [r[...].astype(BF16) for r in ckv_refs]
    ss = [lax.dot_general(q_lat, c, (((1,), (1,)), ((), ())), preferred_element_type=F32)
          + lax.dot_general(q_pe, r[...].astype(BF16), (((1,), (1,)), ((), ())), preferred_element_type=F32)
          for c, r in zip(cs, kpe_refs)]
    s = jnp.concatenate(ss, axis=-1)
    m_old = m_scr[...]
    m_new = jnp.maximum(m_old, jnp.max(s, -1, keepdims=True))
    alpha = jnp.exp(m_old - m_new)
    p = jnp.exp(s - m_new)
    l_scr[...] = alpha * l_scr[...] + jnp.sum(p, -1, keepdims=True)
    acc = alpha * acc_scr[...]
    for n, c in enumerate(cs):
        acc = acc + jnp.dot(p[:, n * PAGE_SIZE:(n + 1) * PAGE_SIZE].astype(BF16), c, preferred_element_type=F32)
    acc_scr[...] = acc
    m_scr[...] = m_new

    @pl.when(s_idx == pl.num_programs(1) - 1)
    def _():
        cn = cnew_ref[...].astype(BF16).astype(F32)
        pn = pnew_ref[...].astype(BF16).astype(F32)
        s_new = (jnp.sum(q_lat.astype(F32) * cn, -1, keepdims=True)
                 + jnp.sum(q_pe.astype(F32) * pn, -1, keepdims=True))
        m_fin = jnp.maximum(m_scr[...], s_new)
        a_fin = jnp.exp(m_scr[...] - m_fin)
        p_new = jnp.exp(s_new - m_fin)
        l_fin = a_fin * l_scr[...] + p_new
        o_ref[...] = (a_fin * acc_scr[...] + p_new.astype(BF16).astype(F32) * cn) / l_fin


def _mla_sample_attend(layer, q_s, c_new, p_new, cache_ckv, cache_kpe, page_table, npg=16):
    dbsz, n_pages = page_table.shape
    hp = q_s.shape[1]

    def page_map(n):
        return lambda b, s, pt: (layer, pt[b, s * npg + n], 0, 0)

    grid_spec = pltpu.PrefetchScalarGridSpec(
        num_scalar_prefetch=1,
        grid=(dbsz, n_pages // npg),
        in_specs=[pl.BlockSpec((None, hp, MLA_QK), lambda b, s, pt: (b, 0, 0)),
                  pl.BlockSpec((None, 1, MLA_KV_RANK), lambda b, s, pt: (b, 0, 0)),
                  pl.BlockSpec((None, 1, MLA_ROPE), lambda b, s, pt: (b, 0, 0))]
                 + [pl.BlockSpec((None, None, PAGE_SIZE, MLA_KV_RANK), page_map(n)) for n in range(npg)]
                 + [pl.BlockSpec((None, None, PAGE_SIZE, MLA_ROPE), page_map(n)) for n in range(npg)],
        out_specs=pl.BlockSpec((None, hp, MLA_KV_RANK), lambda b, s, pt: (b, 0, 0)),
        scratch_shapes=[pltpu.VMEM((hp, 1), F32), pltpu.VMEM((hp, 1), F32), pltpu.VMEM((hp, MLA_KV_RANK), F32)],
    )
    return pl.pallas_call(
        functools.partial(_mla_sample_kernel, npg),
        grid_spec=grid_spec,
        out_shape=jax.ShapeDtypeStruct((dbsz, hp, MLA_KV_RANK), F32),
        compiler_params=_cp("parallel", "arbitrary"),
        name="mla_sample_attend",
    )(page_table, q_s, c_new, p_new, *([cache_ckv] * npg), *([cache_kpe] * npg))


def _mla_vup_kernel(o_ref, wv_ref, y_ref):
    ys = [_dot(o_ref[h], wv_ref[h]) for h in range(MLA_HEADS)]
    y_ref[...] = jnp.concatenate(ys, axis=-1).astype(y_ref.dtype)


def _mla_vup(o_heads, wv):
    hp, dbsz, _ = o_heads.shape
    return pl.pallas_call(
        _mla_vup_kernel,
        grid=(1,),
        in_specs=[pl.BlockSpec((hp, dbsz, MLA_KV_RANK), lambda i: (0, 0, 0)),
                  pl.BlockSpec((MLA_HEADS, MLA_KV_RANK, MLA_V), lambda i: (0, 0, 0))],
        out_specs=pl.BlockSpec((dbsz, MLA_WIDTH), lambda i: (0, 0)),
        out_shape=jax.ShapeDtypeStruct((dbsz, MLA_WIDTH), BF16),
        compiler_params=_cp("arbitrary"),
        name="mla_vup",
    )(o_heads, wv)


def _rope_tables(seqlen, bsz, dbsz, past_len):
    half = MLA_ROPE // 2
    inv = ROPE_THETA ** (-jnp.arange(half, dtype=F32) / half)
    pos = jnp.concatenate([jnp.tile(jnp.arange(seqlen), bsz), jnp.full((dbsz,), past_len)]).astype(F32)
    ang = pos[:, None] * inv[None, :]
    cos = jnp.tile(jnp.cos(ang), (1, 2 * MLA_HEADS))
    sin = jnp.tile(jnp.sin(ang), (1, 2 * MLA_HEADS))
    return cos, sin


def _pad_lanes(v, width=LANES):
    return jnp.pad(v, (0, width - v.shape[0])).reshape(1, width)


def _rotate_half_cols(w):
    half = MLA_ROPE // 2
    return jnp.concatenate([-w[..., half:], w[..., :half]], axis=-1)


def kernel(x_prompt, x_sample, cache_ckv, cache_kpe, page_table, state_ssm, state_conv, state_wkv, state_shift,
           w_in, w_out, ln1_g, ln1_b, ln2_g, ln2_b,
           ssd_conv_w, ssd_conv_b, ssd_dt_bias, ssd_a_log, ssd_d, ssd_norm,
           mla_q_norm, mla_kv_norm, mla_q_up, mla_kv_up,
           rw_mu, rw_w0, rw_w_up, rw_a0, rw_a_up, rw_g_up, rw_k_k, rw_k_a, rw_r_k, rw_ln_g, rw_ln_b,
           ffn_gu, ffn_down, moe_router, moe_gu, moe_down):
    bsz, seqlen, d = x_prompt.shape
    dbsz = x_sample.shape[0]
    n_p = bsz * seqlen
    past_len = page_table.shape[1] * PAGE_SIZE
    x = jnp.concatenate([x_prompt.reshape(n_p, d), x_sample.reshape(dbsz, d)], axis=0)
    t = x.shape[0]
    tm = 320
    cos5, sin5 = _rope_tables(seqlen, bsz, dbsz, past_len)
    row = lambda v: v.reshape(1, -1)

    outs_p = [[] for _ in range(6)]
    outs_s = [[] for _ in range(6)]
    for l in range(DEPTH):
        wi = w_in[l]
        m0 = SSD_IN
        r0 = SSD_IN + MLA_IN
        w_ssd = jnp.pad(wi[:, :SSD_IN], ((0, 0), (0, SSD_UP - SSD_IN))).astype(BF16)
        k_rope_w = wi[:, m0 + 1024:m0 + MLA_IN]
        w_mla = jnp.concatenate([wi[:, m0:m0 + MLA_IN], _rotate_half_cols(k_rope_w)], axis=1).astype(BF16)
        w_rw = wi[:, r0:].astype(BF16)
        q_up = mla_q_up[l]
        wqn = q_up[:, :, :MLA_NOPE].reshape(MLA_Q_RANK, -1).astype(BF16)
        wqr = q_up[:, :, MLA_NOPE:].reshape(MLA_Q_RANK, -1).astype(BF16)
        wqx = _rotate_half_cols(q_up[:, :, MLA_NOPE:]).reshape(MLA_Q_RANK, -1).astype(BF16)
        kv_up = mla_kv_up[l]
        wkn = jnp.transpose(kv_up[:, :, :MLA_NOPE], (1, 2, 0)).astype(BF16)
        wv = jnp.transpose(kv_up[:, :, MLA_NOPE:], (1, 0, 2)).astype(BF16)
        ssd_par = (ssd_conv_w[l], row(ssd_conv_b[l]), _pad_lanes(ssd_dt_bias[l]), _pad_lanes(ssd_a_log[l]))
        d_rep = row(jnp.repeat(ssd_d[l], HEAD_DIM))
        a_rep = row(jnp.repeat(-jnp.exp(ssd_a_log[l]), HEAD_DIM))
        rw_par = (row(rw_mu[l]), row(rw_w0[l]), rw_w_up[l].astype(BF16), row(rw_a0[l]), rw_a_up[l].astype(BF16),
                  rw_g_up[l].astype(BF16), row(rw_k_k[l]), row(rw_k_a[l]))
        rw_post = (row(rw_r_k[l]), row(rw_ln_g[l]), row(rw_ln_b[l]))

        u_ssd = _matmul(x, w_ssd, tm)
        u_mla = _matmul(x, w_mla, tm)
        u_rw = _matmul(x, w_rw, tm)

        y_ssd_p, ssm_p = _ssd_prompt(u_ssd, bsz, seqlen, *ssd_par, d_rep, row(ssd_norm[l]))
        y_ssd_s, ssm_s = _ssd_sample(u_ssd, n_p, dbsz, state_conv[l].reshape(dbsz, -1), state_ssm[l],
                                     *ssd_par, a_rep, d_rep, row(ssd_norm[l]))
        xbc = u_ssd[:, SSD_WIDTH:SSD_WIDTH + SSD_XBC]
        conv_p = xbc[:n_p].reshape(bsz, seqlen, SSD_XBC)[:, seqlen - (SSD_CONV - 1):]
        conv_s = jnp.concatenate([state_conv[l][:, 1:], xbc[n_p:, None, :]], axis=1)

        ckv, kpe, q, kcat = _mla_proj(u_mla, cos5, sin5, row(mla_q_norm[l]), row(mla_kv_norm[l]),
                                      wqn, wqr, wqx, wkn, tm)
        y_mla_p = _mla_prompt_attend(q, kcat, wv, bsz, seqlen)
        q_s = jnp.pad(jnp.transpose(q[:, n_p:], (1, 0, 2)), ((0, 0), (0, 8 - MLA_HEADS), (0, 0)))
        o_lat = _mla_sample_attend(l, q_s, ckv[n_p:, None, :], kpe[n_p:, None, :], cache_ckv, cache_kpe, page_table)
        y_mla_s = _mla_vup(jnp.transpose(o_lat, (1, 0, 2)), wv)

        seq_p = _rw_prep_prompt(u_rw, n_p, seqlen, rw_par)
        y_rw_p, wkv_p = _rw_scan_prompt(seq_p, bsz, seqlen, *rw_post)
        seq_s = _rw_prep_sample(u_rw, n_p, dbsz, state_shift[l], rw_par)
        y_rw_s, wkv_s = _rw_step_sample(seq_s, state_wkv[l], dbsz, *rw_post)
        shift_p = u_rw[:n_p].reshape(bsz, seqlen, RWKV_IN)[:, -1]
        shift_s = u_rw[n_p:]

        mix_in = jnp.concatenate([jnp.concatenate([y_ssd_p, y_mla_p, y_rw_p], axis=-1),
                                  jnp.concatenate([y_ssd_s, y_mla_s, y_rw_s], axis=-1)], axis=0)
        x = _matmul_res_ln(mix_in, w_out[l].astype(BF16), x, row(ln1_g[l]), row(ln1_b[l]), tm)
        if l % 2 == 1:
            gates = _router(x, jnp.pad(moe_router[l // 2], ((0, 0), (0, LANES - N_EXPERTS))), tm)
            x = _moe_dense(x, gates, moe_gu[l // 2].astype(BF16), moe_down[l // 2].astype(BF16),
                           row(ln2_g[l]), row(ln2_b[l]), 640, 256)
        else:
            x = _ffn_dense(x, ffn_gu[l // 2].astype(BF16), ffn_down[l // 2].astype(BF16),
                           row(ln2_g[l]), row(ln2_b[l]), 640, 512)

        for lst, val in zip(outs_p, (ckv[:n_p].reshape(bsz, seqlen, -1), kpe[:n_p].reshape(bsz, seqlen, -1),
                                     ssm_p, conv_p, wkv_p, shift_p)):
            lst.append(val)
        for lst, val in zip(outs_s, (ckv[n_p:].reshape(dbsz, 1, -1), kpe[n_p:].reshape(dbsz, 1, -1),
                                     ssm_s, conv_s, wkv_s, shift_s)):
            lst.append(val)

    ckv_p, kpe_p, ssm_p, conv_p, wkv_p, shift_p = [jnp.stack(o) for o in outs_p]
    ckv_s, kpe_s, ssm_s, conv_s, wkv_s, shift_s = [jnp.stack(o) for o in outs_s]
    y_p = x[:n_p].reshape(bsz, seqlen, d)
    y_s = x[n_p:].reshape(dbsz, 1, d)
    return (y_p, y_s, ckv_p, kpe_p, ckv_s, kpe_s, ssm_p, ssm_s, conv_p, conv_s, wkv_p, wkv_s, shift_p, shift_s)
```

```python
import functools
import math

import jax
import jax.numpy as jnp
from jax import lax
from jax.experimental import pallas as pl
from jax.experimental.pallas import tpu as pltpu

F32 = jnp.float32
BF16 = jnp.bfloat16
HI = lax.Precision.HIGHEST

D_MODEL = 2048
PAGE_SIZE = 128
HEAD_DIM = 64
SSD_WIDTH = 768
SSD_HEADS = 12
SSD_GROUPS = 2
SSD_STATE = 128
SSD_CONV = 4
SSD_CHUNK = 128
SSD_XBC = 1280
SSD_IN = 2060
SSD_GW = SSD_WIDTH // SSD_GROUPS
SSD_UP = 2176
MLA_HEADS = 5
MLA_NOPE = 128
MLA_ROPE = 64
MLA_V = 128
MLA_WIDTH = 640
MLA_Q_RANK = 512
MLA_KV_RANK = 512
MLA_IN = 1088
MLA_UP = 1152
MLA_QK = 640
MLA_SCALE = (MLA_NOPE + MLA_ROPE) ** -0.5
ROPE_THETA = 10000.0
RWKV_WIDTH = 640
RWKV_HEADS = 10
RWKV_IN = 2176
RWKV_GN_EPS = 64e-5
RWKV_CHUNK = 64
RWKV_SUB = 16
FFN_DENSE = 5632
N_EXPERTS = 8
FFN_EXPERT = 2816
DEPTH = 2
ALPHA = (2.0 * DEPTH) ** 0.25
LN_EPS = 1e-5
RMS_EPS = 1e-6
LANES = 128
NEG_BIG = -1e30

VMEM_LIMIT = 56 * 1024 * 1024


def _cp(*sem):
    return pltpu.CompilerParams(dimension_semantics=sem, vmem_limit_bytes=VMEM_LIMIT)


def _dot(a, b):
    return jnp.dot(a.astype(BF16), b.astype(BF16), preferred_element_type=F32)


def _dot_nt(a, b):
    return lax.dot_general(a.astype(BF16), b.astype(BF16), (((1,), (1,)), ((), ())), preferred_element_type=F32)


def _dot_tn(a, b):
    return lax.dot_general(a.astype(BF16), b.astype(BF16), (((0,), (0,)), ((), ())), preferred_element_type=F32)


def _dot_hi(a, b):
    return jnp.dot(a, b, precision=HI, preferred_element_type=F32)


_NN = (((1,), (0,)), ((), ()))
_NT = (((1,), (1,)), ((), ()))
_TN = (((0,), (0,)), ((), ()))


def _dot3(a, b, dims):
    ah = a.astype(BF16)
    bh = b.astype(BF16)
    al = (a - ah.astype(F32)).astype(BF16)
    bl = (b - bh.astype(F32)).astype(BF16)
    out = lax.dot_general(ah, bh, dims, preferred_element_type=F32)
    out = out + lax.dot_general(ah, bl, dims, preferred_element_type=F32)
    return out + lax.dot_general(al, bh, dims, preferred_element_type=F32)


def _sigmoid(x):
    return 1.0 / (1.0 + jnp.exp(-x))


def _silu(x):
    return x * _sigmoid(x)


def _softplus(x):
    return jnp.maximum(x, 0.0) + jnp.log1p(jnp.exp(-jnp.abs(x)))


def _iota(shape, dim):
    return lax.broadcasted_iota(jnp.int32, shape, dim)


def _seg_sum(x, width, seg):
    same = (_iota((width, width), 0) // seg) == (_iota((width, width), 1) // seg)
    ones = jnp.where(same, 1.0, 0.0).astype(BF16)
    hi = x.astype(BF16)
    lo = (x - hi.astype(F32)).astype(BF16)
    return (jnp.dot(hi, ones, preferred_element_type=F32) + jnp.dot(lo, ones, preferred_element_type=F32))


def _layer_norm(x, g, b):
    mu = jnp.mean(x, -1, keepdims=True)
    xc = x - mu
    var = jnp.mean(xc * xc, -1, keepdims=True)
    return xc * lax.rsqrt(var + LN_EPS) * g + b


def _rms_norm(x, g):
    return x * lax.rsqrt(jnp.mean(x * x, -1, keepdims=True) + RMS_EPS) * g


def _wdot(x, w):
    if w.dtype == BF16:
        return jnp.dot(x.astype(BF16), w, preferred_element_type=F32)
    return jnp.dot(x.astype(F32), w, precision=HI, preferred_element_type=F32)


def _row_tile(rows, pref):
    return pref if rows % pref == 0 else rows


def _mm_kernel(x_ref, w_ref, o_ref):
    o_ref[...] = _wdot(x_ref[...], w_ref[...]).astype(o_ref.dtype)


def _matmul(x, w, tm, out_dtype=F32):
    m, k = x.shape
    n = w.shape[1]
    tm = _row_tile(m, tm)
    return pl.pallas_call(
        _mm_kernel,
        grid=(m // tm,),
        in_specs=[pl.BlockSpec((tm, k), lambda i: (i, 0)), pl.BlockSpec((k, n), lambda i: (0, 0))],
        out_specs=pl.BlockSpec((tm, n), lambda i: (i, 0)),
        out_shape=jax.ShapeDtypeStruct((m, n), out_dtype),
        compiler_params=_cp("parallel"),
        name="matmul",
    )(x, w)


def _mm_res_ln_kernel(a_ref, w_ref, res_ref, g_ref, b_ref, o_ref):
    mix = _wdot(a_ref[...], w_ref[...])
    o_ref[...] = _layer_norm(ALPHA * res_ref[...] + mix, g_ref[...], b_ref[...])


def _matmul_res_ln(a, w, res, g, b, tm):
    m, k = a.shape
    n = w.shape[1]
    tm = _row_tile(m, tm)
    return pl.pallas_call(
        _mm_res_ln_kernel,
        grid=(m // tm,),
        in_specs=[pl.BlockSpec((tm, k), lambda i: (i, 0)), pl.BlockSpec((k, n), lambda i: (0, 0)),
                  pl.BlockSpec((tm, n), lambda i: (i, 0)), pl.BlockSpec((1, n), lambda i: (0, 0)),
                  pl.BlockSpec((1, n), lambda i: (0, 0))],
        out_specs=pl.BlockSpec((tm, n), lambda i: (i, 0)),
        out_shape=jax.ShapeDtypeStruct((m, n), F32),
        compiler_params=_cp("parallel"),
        name="out_proj_ln",
    )(a, w, res, g, b)


def _ffn_kernel(x_ref, wg_ref, wu_ref, wd_ref, g_ref, b_ref, o_ref, xb_scr):
    f = pl.program_id(1)

    @pl.when(f == 0)
    def _():
        xb_scr[...] = x_ref[...].astype(xb_scr.dtype)
        o_ref[...] = ALPHA * x_ref[...]

    xb = xb_scr[...]
    act = _silu(_wdot(xb, wg_ref[...])) * _wdot(xb, wu_ref[...])
    o_ref[...] += _wdot(act, wd_ref[...])

    @pl.when(f == pl.num_programs(1) - 1)
    def _():
        o_ref[...] = _layer_norm(o_ref[...], g_ref[...], b_ref[...])


def _ffn_dense(x, w_gu, w_down, g, b, tm, tf):
    m, d = x.shape
    fdim = w_down.shape[0]
    nf = fdim // tf
    tm = _row_tile(m, tm)
    return pl.pallas_call(
        _ffn_kernel,
        grid=(m // tm, nf),
        in_specs=[pl.BlockSpec((tm, d), lambda i, f: (i, 0)),
                  pl.BlockSpec((d, tf), lambda i, f: (0, f)),
                  pl.BlockSpec((d, tf), lambda i, f: (0, nf + f)),
                  pl.BlockSpec((tf, d), lambda i, f: (f, 0)),
                  pl.BlockSpec((1, d), lambda i, f: (0, 0)),
                  pl.BlockSpec((1, d), lambda i, f: (0, 0))],
        out_specs=pl.BlockSpec((tm, d), lambda i, f: (i, 0)),
        out_shape=jax.ShapeDtypeStruct((m, d), F32),
        scratch_shapes=[pltpu.VMEM((tm, d), w_gu.dtype)],
        compiler_params=_cp("parallel", "arbitrary"),
        name="ffn_dense",
    )(x, w_gu, w_gu, w_down, g, b)


def _router_kernel(x_ref, wr_ref, gates_ref):
    logits = _dot_hi(x_ref[...], wr_ref[...])
    lane = _iota(logits.shape, 1)
    valid = lane < N_EXPERTS
    logits = jnp.where(valid, logits, NEG_BIG)
    mx = jnp.max(logits, -1, keepdims=True)
    ex = jnp.where(valid, jnp.exp(logits - mx), 0.0)
    probs = ex / jnp.sum(ex, -1, keepdims=True)
    p1 = jnp.max(probs, -1, keepdims=True)
    i1 = jnp.min(jnp.where(probs == p1, lane, LANES), -1, keepdims=True)
    first = lane == i1
    rest = jnp.where(first | (~valid), -1.0, probs)
    p2 = jnp.max(rest, -1, keepdims=True)
    i2 = jnp.min(jnp.where(rest == p2, lane, LANES), -1, keepdims=True)
    second = lane == i2
    tot = p1 + p2
    gates_ref[...] = jnp.where(first, p1 / tot, 0.0) + jnp.where(second, p2 / tot, 0.0)


def _router(x, wr_pad, tm):
    m, d = x.shape
    tm = _row_tile(m, tm)
    return pl.pallas_call(
        _router_kernel,
        grid=(m // tm,),
        in_specs=[pl.BlockSpec((tm, d), lambda i: (i, 0)), pl.BlockSpec((d, LANES), lambda i: (0, 0))],
        out_specs=pl.BlockSpec((tm, LANES), lambda i: (i, 0)),
        out_shape=jax.ShapeDtypeStruct((m, LANES), F32),
        compiler_params=_cp("parallel"),
        name="moe_router",
    )(x, wr_pad)


def _moe_kernel(x_ref, gates_ref, wg_ref, wu_ref, wd_ref, g_ref, b_ref, o_ref, xb_scr):
    e = pl.program_id(1)
    f = pl.program_id(2)

    @pl.when((e == 0) & (f == 0))
    def _():
        xb_scr[...] = x_ref[...].astype(BF16)
        o_ref[...] = ALPHA * x_ref[...]

    gates = gates_ref[...]
    gate = jnp.sum(jnp.where(_iota(gates.shape, 1) == e, gates, 0.0), -1, keepdims=True)
    xb = xb_scr[...]
    hg = jnp.dot(xb, wg_ref[...], preferred_element_type=F32)
    hu = jnp.dot(xb, wu_ref[...], preferred_element_type=F32)
    act = (_silu(hg) * hu * gate).astype(BF16)
    o_ref[...] += jnp.dot(act, wd_ref[...], preferred_element_type=F32)

    @pl.when((e == pl.num_programs(1) - 1) & (f == pl.num_programs(2) - 1))
    def _():
        o_ref[...] = _layer_norm(o_ref[...], g_ref[...], b_ref[...])


def _moe_dense(x, gates, w_gu, w_down, g, b, tm, tf):
    m, d = x.shape
    ne, fdim, _ = w_down.shape
    tm = _row_tile(m, tm)
    nf = fdim // tf
    return pl.pallas_call(
        _moe_kernel,
        grid=(m // tm, ne, nf),
        in_specs=[pl.BlockSpec((tm, d), lambda i, e, f: (i, 0)),
                  pl.BlockSpec((tm, LANES), lambda i, e, f: (i, 0)),
                  pl.BlockSpec((None, d, tf), lambda i, e, f: (e, 0, f)),
                  pl.BlockSpec((None, d, tf), lambda i, e, f: (e, 0, nf + f)),
                  pl.BlockSpec((None, tf, d), lambda i, e, f: (e, f, 0)),
                  pl.BlockSpec((1, d), lambda i, e, f: (0, 0)),
                  pl.BlockSpec((1, d), lambda i, e, f: (0, 0))],
        out_specs=pl.BlockSpec((tm, d), lambda i, e, f: (i, 0)),
        out_shape=jax.ShapeDtypeStruct((m, d), F32),
        scratch_shapes=[pltpu.VMEM((tm, d), BF16)],
        compiler_params=_cp("parallel", "arbitrary", "arbitrary"),
        name="moe_dense",
    )(x, gates, w_gu, w_gu, w_down, g, b)


def _ssd_gate_norm(y, xs, z, d_rep, norm_w):
    y = (y + d_rep * xs) * _silu(z)
    outs = []
    for g in range(SSD_GROUPS):
        sl = slice(g * SSD_GW, (g + 1) * SSD_GW)
        outs.append(_rms_norm(y[:, sl], norm_w[:, sl]))
    return jnp.concatenate(outs, axis=-1)


def _ssd_prompt_kernel(u_ref, cw_ref, cb_ref, dtb_ref, alog_ref, drep_ref, nw_ref,
                       y_ref, hout_ref, xf_scr, h_scr):
    q = SSD_CHUNK
    c = pl.program_id(1)

    @pl.when(c == 0)
    def _():
        xf_scr[0:8, :] = jnp.zeros((8, SSD_XBC), F32)
        h_scr[...] = jnp.zeros_like(h_scr)

    xf_scr[8:8 + q, :] = u_ref[:, SSD_WIDTH:SSD_WIDTH + SSD_XBC]
    acc = cb_ref[...] + xf_scr[5:5 + q, :] * cw_ref[0:1, :]
    for k in range(1, SSD_CONV):
        acc = acc + xf_scr[5 + k:5 + k + q, :] * cw_ref[k:k + 1, :]
    xf_scr[5:8, :] = xf_scr[5 + q:8 + q, :]
    xbc = _silu(acc)
    xs = xbc[:, :SSD_WIDTH]
    bmat = xbc[:, SSD_WIDTH:SSD_WIDTH + SSD_GROUPS * SSD_STATE]
    cmat = xbc[:, SSD_WIDTH + SSD_GROUPS * SSD_STATE:]

    dt = _softplus(u_ref[:, SSD_WIDTH + SSD_XBC:] + dtb_ref[...])
    da = dt * (-jnp.exp(alog_ref[...]))
    row = _iota((q, q), 0)
    col = _iota((q, q), 1)
    causal = row >= col
    acum = _dot_hi(jnp.where(causal, 1.0, 0.0), da)
    dt_t = dt.T
    acum_t = _dot_hi(da.T, jnp.where(row <= col, 1.0, 0.0))

    scores = [_dot_nt(cmat[:, g * SSD_STATE:(g + 1) * SSD_STATE], bmat[:, g * SSD_STATE:(g + 1) * SSD_STATE])
              for g in range(SSD_GROUPS)]
    rep = SSD_HEADS // SSD_GROUPS
    heads = range(SSD_HEADS)
    gsl = [slice((h // rep) * SSD_STATE, (h // rep + 1) * SSD_STATE) for h in heads]
    a_col = [acum[:, h:h + 1] for h in heads]
    a_last = [acum[q - 1:q, h:h + 1] for h in heads]
    xs_h = [xs[:, h * HEAD_DIM:(h + 1) * HEAD_DIM] for h in heads]
    h_prev = [h_scr[h] for h in heads]
    m = [scores[h // rep] * jnp.exp(jnp.where(causal, a_col[h] - acum_t[h:h + 1, :], NEG_BIG)) * dt_t[h:h + 1, :]
         for h in heads]
    y_diag = [_dot(m[h], xs_h[h]) for h in heads]
    y_off = [jnp.exp(a_col[h]) * _dot_nt(cmat[:, gsl[h]], h_prev[h]) for h in heads]
    xw = [xs_h[h] * (jnp.exp(a_last[h] - a_col[h]) * dt[:, h:h + 1]) for h in heads]
    h_new = [jnp.exp(a_last[h]) * h_prev[h] + _dot_tn(xw[h], bmat[:, gsl[h]]) for h in heads]
    for h in heads:
        h_scr[h] = h_new[h]
    y_all = jnp.concatenate([y_diag[h] + y_off[h] for h in heads], axis=-1)

    y_ref[...] = _ssd_gate_norm(y_all, xs, u_ref[:, :SSD_WIDTH], drep_ref[...], nw_ref[...]).astype(y_ref.dtype)

    @pl.when(c == pl.num_programs(1) - 1)
    def _():
        hout_ref[...] = h_scr[...]


def _ssd_prompt(u_ssd, bsz, seqlen, cw, cb, dtb, alog, drep, nw):
    q = SSD_CHUNK
    nc = seqlen // q
    full = lambda shape: pl.BlockSpec(shape, lambda b, c: (0,) * len(shape))
    return pl.pallas_call(
        _ssd_prompt_kernel,
        grid=(bsz, nc),
        in_specs=[pl.BlockSpec((q, SSD_UP), lambda b, c: (b * nc + c, 0)),
                  full((SSD_CONV, SSD_XBC)), full((1, SSD_XBC)), full((1, LANES)), full((1, LANES)),
                  full((1, SSD_WIDTH)), full((1, SSD_WIDTH))],
        out_specs=[pl.BlockSpec((q, SSD_WIDTH), lambda b, c: (b * nc + c, 0)),
                   pl.BlockSpec((None, SSD_HEADS, HEAD_DIM, SSD_STATE), lambda b, c: (b, 0, 0, 0))],
        out_shape=[jax.ShapeDtypeStruct((bsz * seqlen, SSD_WIDTH), BF16),
                   jax.ShapeDtypeStruct((bsz, SSD_HEADS, HEAD_DIM, SSD_STATE), F32)],
        scratch_shapes=[pltpu.VMEM((8 + q, SSD_XBC), F32),
                        pltpu.VMEM((SSD_HEADS, HEAD_DIM, SSD_STATE), F32)],
        compiler_params=_cp("parallel", "arbitrary"),
        name="ssd_prompt",
    )(u_ssd, cw, cb, dtb, alog, drep, nw)


def _ssd_sample_kernel(u_ref, cbuf_ref, h0_ref, cw_ref, cb_ref, dtb_ref, alog_ref, arep_ref, drep_ref, nw_ref,
                       y_ref, hout_ref):
    bt = u_ref.shape[0]
    x_new = u_ref[:, SSD_WIDTH:SSD_WIDTH + SSD_XBC]
    acc = cb_ref[...] + x_new * cw_ref[SSD_CONV - 1:SSD_CONV, :]
    for k in range(SSD_CONV - 1):
        acc = acc + cbuf_ref[:, k * SSD_XBC:(k + 1) * SSD_XBC] * cw_ref[k:k + 1, :]
    xbc = _silu(acc)
    xs = xbc[:, :SSD_WIDTH]
    bmat = xbc[:, SSD_WIDTH:SSD_WIDTH + SSD_GROUPS * SSD_STATE]
    cmat = xbc[:, SSD_WIDTH + SSD_GROUPS * SSD_STATE:]
    dt = _softplus(u_ref[:, SSD_WIDTH + SSD_XBC:] + dtb_ref[...])
    expand = jnp.where(_iota((LANES, SSD_WIDTH), 0) == _iota((LANES, SSD_WIDTH), 1) // HEAD_DIM, 1.0, 0.0)
    dt_rep = _dot_hi(dt, expand)
    dec_rep = jnp.exp(dt_rep * arep_ref[...])
    dtx = dt_rep * xs
    rows = _iota((bt, 1), 0)
    ones = jnp.ones((bt, SSD_STATE), F32)
    rep = SSD_HEADS // SSD_GROUPS
    seqs = range(bt)
    sel = [rows == bi for bi in seqs]
    y_cols = []
    for g in range(SSD_GROUPS):
        cs = slice(g * SSD_GW, (g + 1) * SSD_GW)
        gs = slice(g * SSD_STATE, (g + 1) * SSD_STATE)
        hsl = slice(g * rep, (g + 1) * rep)
        h0 = [h0_ref[bi, hsl].reshape(SSD_GW, SSD_STATE) for bi in seqs]
        outer = [_dot3(jnp.where(sel[bi], dtx[:, cs], 0.0), bmat[:, gs], _TN) for bi in seqs]
        decm = [_dot3(jnp.where(sel[bi], dec_rep[:, cs], 0.0), ones, _TN) for bi in seqs]
        hn = [decm[bi] * h0[bi] + outer[bi] for bi in seqs]
        for bi in seqs:
            hout_ref[bi, hsl] = hn[bi].reshape(rep, HEAD_DIM, SSD_STATE)
        ys = [_dot3(jnp.where(sel[bi], cmat[:, gs], 0.0), hn[bi], _NT) for bi in seqs]
        y_cols.append(functools.reduce(lambda a, b: a + b, ys))
    y_all = jnp.concatenate(y_cols, axis=-1)
    y_ref[...] = _ssd_gate_norm(y_all, xs, u_ref[:, :SSD_WIDTH], drep_ref[...], nw_ref[...]).astype(y_ref.dtype)


def _ssd_sample(layer, u_ssd, row0, dbsz, conv_flat, h0, cw, cb, dtb, alog, arep, drep, nw, bt=8):
    full = lambda shape: pl.BlockSpec(shape, lambda i: (0,) * len(shape))
    blk0 = row0 // bt
    return pl.pallas_call(
        _ssd_sample_kernel,
        grid=(dbsz // bt,),
        in_specs=[pl.BlockSpec((bt, SSD_UP), lambda i: (blk0 + i, 0)),
                  pl.BlockSpec((bt, (SSD_CONV - 1) * SSD_XBC), lambda i: (i, 0)),
                  pl.BlockSpec((None, bt, SSD_HEADS, HEAD_DIM, SSD_STATE), lambda i: (layer, i, 0, 0, 0)),
                  full((SSD_CONV, SSD_XBC)), full((1, SSD_XBC)), full((1, LANES)), full((1, LANES)),
                  full((1, SSD_WIDTH)), full((1, SSD_WIDTH)), full((1, SSD_WIDTH))],
        out_specs=[pl.BlockSpec((bt, SSD_WIDTH), lambda i: (i, 0)),
                   pl.BlockSpec((bt, SSD_HEADS, HEAD_DIM, SSD_STATE), lambda i: (i, 0, 0, 0))],
        out_shape=[jax.ShapeDtypeStruct((dbsz, SSD_WIDTH), F32),
                   jax.ShapeDtypeStruct((dbsz, SSD_HEADS, HEAD_DIM, SSD_STATE), F32)],
        compiler_params=_cp("parallel"),
        name="ssd_sample",
    )(u_ssd, conv_flat, h0, cw, cb, dtb, alog, arep, drep, nw)


def _rw_prep_body(u, prev, mu, w0, w_up, a0, a_up, g_up, k_k, k_a):
    o1, o2, o3 = RWKV_WIDTH, 2 * RWKV_WIDTH, 3 * RWKV_WIDTH
    o4, o5 = o3 + 64, o3 + 128
    ux = u + (prev - u) * mu
    r, k, v = ux[:, :o1], ux[:, o1:o2], ux[:, o2:o3]
    xw, xa, xg = ux[:, o3:o4], ux[:, o4:o5], ux[:, o5:]
    w = -_softplus(-(w0 + _wdot(jnp.tanh(xw), w_up))) - 0.5
    lw = -jnp.exp(w)
    a = _sigmoid(a0 + _wdot(xa, a_up))
    g = _wdot(_sigmoid(xg), g_up)
    kk = k * k_k
    nrm = jnp.maximum(jnp.sqrt(_seg_sum(kk * kk, RWKV_WIDTH, HEAD_DIM)), 1e-12)
    kap = kk / nrm
    k = k * (1.0 + (a - 1.0) * k_a)
    return r, lw, k, v, kap, kap * a, g


def _rw_prep_prompt_kernel(tiles_per_seq, u_ref, mu_ref, w0_ref, wup_ref, a0_ref, aup_ref, gup_ref, kk_ref, ka_ref,
                           r_ref, lw_ref, k_ref, v_ref, kap_ref, beta_ref, g_ref, scr):
    tm = u_ref.shape[0]
    i = pl.program_id(0)

    @pl.when(i % tiles_per_seq == 0)
    def _():
        scr[0:8, :] = jnp.zeros((8, RWKV_IN), F32)

    scr[8:8 + tm, :] = u_ref[...]
    prev = scr[7:7 + tm, :]
    outs = _rw_prep_body(u_ref[...], prev, mu_ref[...], w0_ref[...], wup_ref[...], a0_ref[...], aup_ref[...],
                         gup_ref[...], kk_ref[...], ka_ref[...])
    scr[7:8, :] = scr[7 + tm:8 + tm, :]
    for ref, val in zip((r_ref, lw_ref, k_ref, v_ref, kap_ref, beta_ref, g_ref), outs):
        ref[...] = val


def _rw_prep_sample_kernel(u_ref, prev_ref, mu_ref, w0_ref, wup_ref, a0_ref, aup_ref, gup_ref, kk_ref, ka_ref,
                           r_ref, lw_ref, k_ref, v_ref, kap_ref, beta_ref, g_ref):
    outs = _rw_prep_body(u_ref[...], prev_ref[...], mu_ref[...], w0_ref[...], wup_ref[...], a0_ref[...],
                         aup_ref[...], gup_ref[...], kk_ref[...], ka_ref[...])
    for ref, val in zip((r_ref, lw_ref, k_ref, v_ref, kap_ref, beta_ref, g_ref), outs):
        ref[...] = val


def _rw_param_specs(nargs):
    full = lambda shape: pl.BlockSpec(shape, lambda *a: (0,) * len(shape))
    del nargs
    return [full((1, RWKV_IN)), full((1, RWKV_WIDTH)), full((64, RWKV_WIDTH)), full((1, RWKV_WIDTH)),
            full((64, RWKV_WIDTH)), full((128, RWKV_WIDTH)), full((1, RWKV_WIDTH)), full((1, RWKV_WIDTH))]


def _rw_prep_prompt(u_rw, nrows, seqlen, params, tm=256):
    outs = [jax.ShapeDtypeStruct((nrows, RWKV_WIDTH), F32)] * 7
    return pl.pallas_call(
        functools.partial(_rw_prep_prompt_kernel, seqlen // tm),
        grid=(nrows // tm,),
        in_specs=[pl.BlockSpec((tm, RWKV_IN), lambda i: (i, 0))] + _rw_param_specs(1),
        out_specs=[pl.BlockSpec((tm, RWKV_WIDTH), lambda i: (i, 0))] * 7,
        out_shape=outs,
        scratch_shapes=[pltpu.VMEM((8 + tm, RWKV_IN), F32)],
        compiler_params=_cp("arbitrary"),
        name="rwkv_prep_prompt",
    )(u_rw, *params)


def _rw_prep_sample(u_rw, row0, dbsz, prev, params):
    outs = [jax.ShapeDtypeStruct((dbsz, RWKV_WIDTH), F32)] * 7
    return pl.pallas_call(
        _rw_prep_sample_kernel,
        grid=(1,),
        in_specs=[pl.BlockSpec((dbsz, RWKV_IN), lambda i: (row0 // dbsz, 0)),
                  pl.BlockSpec((dbsz, RWKV_IN), lambda i: (0, 0))] + _rw_param_specs(1),
        out_specs=[pl.BlockSpec((dbsz, RWKV_WIDTH), lambda i: (0, 0))] * 7,
        out_shape=outs,
        compiler_params=_cp("arbitrary"),
        name="rwkv_prep_sample",
    )(u_rw, prev, *params)


def _rw_post(y, r, k, v, g, rk, ln_g, ln_b):
    mu = _seg_sum(y, RWKV_WIDTH, HEAD_DIM) * (1.0 / HEAD_DIM)
    yc = y - mu
    var = _seg_sum(yc * yc, RWKV_WIDTH, HEAD_DIM) * (1.0 / HEAD_DIM)
    yn = yc * lax.rsqrt(var + RWKV_GN_EPS) * ln_g + ln_b
    bonus = _seg_sum(r * k * rk, RWKV_WIDTH, HEAD_DIM) * v
    return (yn + bonus) * g


def _unit_lower_inverse(a, c, sub):
    row = _iota((c, c), 0)
    col = _iota((c, c), 1)
    eye = jnp.where(row == col, 1.0, 0.0)
    blk = (row // sub) == (col // sub)
    d = [jnp.where(blk, m, 0.0) for m in a]
    n = [m - dm for m, dm in zip(a, d)]
    x = [eye - dm for dm in d]
    p = d
    for _ in range(int(math.log2(sub)) - 1):
        p = [_dot3(m, m, _NN) for m in p]
        x = [xm + _dot3(xm, pm, _NN) for xm, pm in zip(x, p)]
    e = [_dot3(xm, nm, _NN) for xm, nm in zip(x, n)]
    y = [eye - em for em in e]
    p = e
    for _ in range(int(math.log2(c // sub)) - 1):
        p = [_dot3(m, m, _NN) for m in p]
        y = [ym + _dot3(ym, pm, _NN) for ym, pm in zip(y, p)]
    return [_dot3(ym, xm, _NN) for ym, xm in zip(y, x)]


def _rw_scan_kernel(r_ref, lw_ref, k_ref, v_ref, kap_ref, beta_ref, g_ref, rk_ref, lng_ref, lnb_ref,
                    y_ref, sout_ref, s_scr, y_scr):
    c = RWKV_CHUNK
    ci = pl.program_id(1)

    @pl.when(ci == 0)
    def _():
        s_scr[...] = jnp.zeros_like(s_scr)

    r, lw, k, v = r_ref[...], lw_ref[...], k_ref[...], v_ref[...]
    tri = jnp.where(_iota((c, c), 0) >= _iota((c, c), 1), 1.0, 0.0)
    logp = _dot_hi(tri, lw)
    p_inv = jnp.exp(-logp)
    p_end = jnp.exp(logp[c - 1:c, :])
    kr = jnp.concatenate([kap_ref[...] * jnp.exp(logp - lw), r * jnp.exp(logp)], axis=0)
    kb = k * p_inv
    bb = beta_ref[...] * p_inv
    hat = jnp.concatenate([kb * p_end, bb * p_end], axis=0)
    row2 = _iota((2 * c, c), 0)
    col2 = _iota((2 * c, c), 1)
    keep = col2 < jnp.where(row2 < c, row2, row2 - c + 1)
    heads = range(RWKV_HEADS)
    hsl = [slice(h * HEAD_DIM, (h + 1) * HEAD_DIM) for h in heads]
    kr_h = [kr[:, s] for s in hsl]
    v_h = [v[:, s] for s in hsl]
    s0 = [s_scr[h] for h in heads]
    g_b = [jnp.where(keep, _dot3(kr_h[h], bb[:, hsl[h]], _NT), 0.0) for h in heads]
    g_k = [jnp.where(keep, _dot3(kr_h[h], kb[:, hsl[h]], _NT), 0.0) for h in heads]
    tinv = _unit_lower_inverse([m[:c] for m in g_b], c, RWKV_SUB)
    z = [_dot3(kr_h[h], s0[h], _NT) + _dot3(g_k[h], v_h[h], _NN) for h in heads]
    u_m = [_dot3(tinv[h], z[h][:c], _NN) for h in heads]
    for h in heads:
        y_scr[:, hsl[h]] = z[h][c:] - _dot3(g_b[h][c:], u_m[h], _NN)
    for h in heads:
        vu = jnp.concatenate([v_h[h], -u_m[h]], axis=0)
        s_scr[h] = s0[h] * p_end[:, hsl[h]] + _dot3(vu, hat[:, hsl[h]], _TN)

    y_ref[...] = _rw_post(y_scr[...], r, k, v, g_ref[...], rk_ref[...], lng_ref[...], lnb_ref[...]).astype(y_ref.dtype)

    @pl.when(ci == pl.num_programs(1) - 1)
    def _():
        sout_ref[...] = s_scr[...]


def _rw_scan_prompt(seqs, bsz, seqlen, rk, ln_g, ln_b):
    c = RWKV_CHUNK
    nc = seqlen // c
    tok = pl.BlockSpec((c, RWKV_WIDTH), lambda b, i: (b * nc + i, 0))
    par = pl.BlockSpec((1, RWKV_WIDTH), lambda b, i: (0, 0))
    return pl.pallas_call(
        _rw_scan_kernel,
        grid=(bsz, nc),
        in_specs=[tok] * 7 + [par] * 3,
        out_specs=[tok, pl.BlockSpec((None, RWKV_HEADS, HEAD_DIM, HEAD_DIM), lambda b, i: (b, 0, 0, 0))],
        out_shape=[jax.ShapeDtypeStruct((bsz * seqlen, RWKV_WIDTH), BF16),
                   jax.ShapeDtypeStruct((bsz, RWKV_HEADS, HEAD_DIM, HEAD_DIM), F32)],
        scratch_shapes=[pltpu.VMEM((RWKV_HEADS, HEAD_DIM, HEAD_DIM), F32), pltpu.VMEM((c, RWKV_WIDTH), F32)],
        compiler_params=_cp("parallel", "arbitrary"),
        name="rwkv_scan_prompt",
    )(*seqs, rk, ln_g, ln_b)


def _rw_step_kernel(r_ref, lw_ref, k_ref, v_ref, kap_ref, beta_ref, g_ref, s0_ref, rk_ref, lng_ref, lnb_ref,
                    y_ref, sout_ref):
    bt = r_ref.shape[0]
    rows = _iota((bt, 1), 0)
    r, k, v = r_ref[...], k_ref[...], v_ref[...]
    dec = jnp.exp(lw_ref[...])
    kap, beta = kap_ref[...], beta_ref[...]
    seqs = range(bt)
    r_b = [jnp.where(rows == bi, r, 0.0) for bi in seqs]
    v_b = [jnp.where(rows == bi, v, 0.0) for bi in seqs]
    y_cols = []
    for h in range(RWKV_HEADS):
        hs = slice(h * HEAD_DIM, (h + 1) * HEAD_DIM)
        s0 = [s0_ref[bi, h] for bi in seqs]
        sa = [jnp.sum(s0[bi] * (-kap[bi:bi + 1, hs]), -1, keepdims=True) for bi in seqs]
        vk = [_dot3(v_b[bi][:, hs], k[:, hs], _TN) for bi in seqs]
        sn = [s0[bi] * dec[bi:bi + 1, hs] + sa[bi] * beta[bi:bi + 1, hs] + vk[bi] for bi in seqs]
        for bi in seqs:
            sout_ref[bi, h] = sn[bi]
        ys = [_dot3(r_b[bi][:, hs], sn[bi], _NT) for bi in seqs]
        y_cols.append(functools.reduce(lambda a, b: a + b, ys))
    y_all = jnp.concatenate(y_cols, axis=-1)
    y_ref[...] = _rw_post(y_all, r, k, v, g_ref[...], rk_ref[...], lng_ref[...], lnb_ref[...]).astype(y_ref.dtype)


def _rw_step_sample(seqs, s0, dbsz, rk, ln_g, ln_b, bt=8):
    tok = pl.BlockSpec((bt, RWKV_WIDTH), lambda i: (i, 0))
    par = pl.BlockSpec((1, RWKV_WIDTH), lambda i: (0, 0))
    st = pl.BlockSpec((bt, RWKV_HEADS, HEAD_DIM, HEAD_DIM), lambda i: (i, 0, 0, 0))
    return pl.pallas_call(
        _rw_step_kernel,
        grid=(dbsz // bt,),
        in_specs=[tok] * 7 + [st] + [par] * 3,
        out_specs=[tok, st],
        out_shape=[jax.ShapeDtypeStruct((dbsz, RWKV_WIDTH), F32),
                   jax.ShapeDtypeStruct((dbsz, RWKV_HEADS, HEAD_DIM, HEAD_DIM), F32)],
        compiler_params=_cp("parallel"),
        name="rwkv_step_sample",
    )(*seqs, s0, rk, ln_g, ln_b)


def _mla_proj_kernel(u_ref, cos_ref, sin_ref, qn_ref, kvn_ref, wqn_ref, wqr_ref, wqx_ref, wkn_ref,
                     ckv_ref, kpe_ref, q_ref, kcat_ref):
    tm = u_ref.shape[0]
    c_q = _rms_norm(u_ref[:, :MLA_Q_RANK], qn_ref[...])
    c_kv = _rms_norm(u_ref[:, MLA_Q_RANK:MLA_Q_RANK + MLA_KV_RANK], kvn_ref[...])
    cos = cos_ref[...]
    sin = sin_ref[...]
    o = MLA_Q_RANK + MLA_KV_RANK
    k_pe = u_ref[:, o:o + MLA_ROPE] * cos[:, :MLA_ROPE] + u_ref[:, o + MLA_ROPE:o + 2 * MLA_ROPE] * sin[:, :MLA_ROPE]
    ckv_ref[...] = c_kv
    kpe_ref[...] = k_pe
    zpad = jnp.zeros((tm, MLA_QK - MLA_KV_RANK - MLA_ROPE), F32)
    kcat_ref[...] = jnp.concatenate([c_kv, k_pe, zpad], axis=-1).astype(BF16)
    q_nope = _wdot(c_q, wqn_ref[...])
    q_pe = _wdot(c_q, wqr_ref[...]) * cos + _wdot(c_q, wqx_ref[...]) * sin
    for h in range(MLA_HEADS):
        q_lat = _wdot(q_nope[:, h * MLA_NOPE:(h + 1) * MLA_NOPE], wkn_ref[h])
        qh = jnp.concatenate([q_lat, q_pe[:, h * MLA_ROPE:(h + 1) * MLA_ROPE], zpad], axis=-1) * MLA_SCALE
        q_ref[h] = qh.astype(q_ref.dtype)


def _mla_proj(u_mla, cos5, sin5, qn, kvn, wqn, wqr, wqx, wkn, tm):
    t = u_mla.shape[0]
    tm = _row_tile(t, tm)
    full = lambda shape: pl.BlockSpec(shape, lambda i: (0,) * len(shape))
    hr = MLA_HEADS * MLA_ROPE
    return pl.pallas_call(
        _mla_proj_kernel,
        grid=(t // tm,),
        in_specs=[pl.BlockSpec((tm, MLA_UP), lambda i: (i, 0)),
                  pl.BlockSpec((tm, hr), lambda i: (i, 0)), pl.BlockSpec((tm, hr), lambda i: (i, 0)),
                  full((1, MLA_Q_RANK)), full((1, MLA_KV_RANK)),
                  full((MLA_Q_RANK, MLA_HEADS * MLA_NOPE)), full((MLA_Q_RANK, hr)), full((MLA_Q_RANK, hr)),
                  full((MLA_HEADS, MLA_NOPE, MLA_KV_RANK))],
        out_specs=[pl.BlockSpec((tm, MLA_KV_RANK), lambda i: (i, 0)),
                   pl.BlockSpec((tm, MLA_ROPE), lambda i: (i, 0)),
                   pl.BlockSpec((MLA_HEADS, tm, MLA_QK), lambda i: (0, i, 0)),
                   pl.BlockSpec((tm, MLA_QK), lambda i: (i, 0))],
        out_shape=[jax.ShapeDtypeStruct((t, MLA_KV_RANK), F32), jax.ShapeDtypeStruct((t, MLA_ROPE), F32),
                   jax.ShapeDtypeStruct((MLA_HEADS, t, MLA_QK), wqn.dtype), jax.ShapeDtypeStruct((t, MLA_QK), BF16)],
        compiler_params=_cp("parallel"),
        name="mla_proj",
    )(u_mla, cos5, sin5, qn, kvn, wqn, wqr, wqx, wkn)


def _mla_prompt_kernel(tq, tk, q_ref, k_ref, wv_ref, y_ref, m_scr, l_scr, acc_scr):
    i = pl.program_id(1)
    j = pl.program_id(2)
    rows = MLA_HEADS * tq
    last = (i * tq + tq - 1) // tk

    @pl.when(j == 0)
    def _():
        m_scr[...] = jnp.full(m_scr.shape, NEG_BIG, F32)
        l_scr[...] = jnp.zeros_like(l_scr)
        acc_scr[...] = jnp.zeros_like(acc_scr)

    @pl.when(j <= last)
    def _():
        q = q_ref[...].reshape(rows, MLA_QK)
        kc = k_ref[...]
        s = lax.dot_general(q, kc, (((1,), (1,)), ((), ())), preferred_element_type=F32)
        qpos = i * tq + _iota((rows, tk), 0) % tq
        kpos = j * tk + _iota((rows, tk), 1)
        s = jnp.where(kpos <= qpos, s, NEG_BIG)
        m_old = m_scr[...]
        m_new = jnp.maximum(m_old, jnp.max(s, -1, keepdims=True))
        alpha = jnp.exp(m_old - m_new)
        p = jnp.exp(s - m_new)
        l_scr[...] = alpha * l_scr[...] + jnp.sum(p, -1, keepdims=True)
        acc_scr[...] = alpha * acc_scr[...] + jnp.dot(p.astype(BF16), kc[:, :MLA_KV_RANK],
                                                      preferred_element_type=F32)
        m_scr[...] = m_new

    @pl.when(j == pl.num_programs(2) - 1)
    def _():
        o = (acc_scr[...] / l_scr[...]).reshape(MLA_HEADS, tq, MLA_KV_RANK)
        ys = [_dot(o[h], wv_ref[h]) for h in range(MLA_HEADS)]
        y_ref[...] = jnp.concatenate(ys, axis=-1).astype(y_ref.dtype)


def _mla_prompt_attend(q, kcat, wv, bsz, seqlen, tq=256, tk=512):
    nq = seqlen // tq
    nk = seqlen // tk
    rows = MLA_HEADS * tq

    def k_map(b, i, j):
        return (b * nk + jnp.minimum(j, (i * tq + tq - 1) // tk), 0)

    return pl.pallas_call(
        functools.partial(_mla_prompt_kernel, tq, tk),
        grid=(bsz, nq, nk),
        in_specs=[pl.BlockSpec((MLA_HEADS, tq, MLA_QK), lambda b, i, j: (0, b * nq + i, 0)),
                  pl.BlockSpec((tk, MLA_QK), k_map),
                  pl.BlockSpec((MLA_HEADS, MLA_KV_RANK, MLA_V), lambda b, i, j: (0, 0, 0))],
        out_specs=pl.BlockSpec((tq, MLA_WIDTH), lambda b, i, j: (b * nq + i, 0)),
        out_shape=jax.ShapeDtypeStruct((bsz * seqlen, MLA_WIDTH), BF16),
        scratch_shapes=[pltpu.VMEM((rows, 1), F32), pltpu.VMEM((rows, 1), F32),
                        pltpu.VMEM((rows, MLA_KV_RANK), F32)],
        compiler_params=_cp("parallel", "parallel", "arbitrary"),
        name="mla_prompt_attend",
    )(q, kcat, wv)


def _mla_sample_kernel(npg, pt_ref, q_ref, cnew_ref, pnew_ref, *refs):
    del pt_ref
    ckv_refs = refs[:npg]
    kpe_refs = refs[npg:2 * npg]
    o_ref = refs[2 * npg]
    m_scr, l_scr, acc_scr = refs[2 * npg + 1:]
    s_idx = pl.program_id(1)

    @pl.when(s_idx == 0)
    def _():
        m_scr[...] = jnp.full(m_scr.shape, NEG_BIG, F32)
        l_scr[...] = jnp.zeros_like(l_scr)
        acc_scr[...] = jnp.zeros_like(acc_scr)

    hp = q_ref.shape[0]

    def split(x):
        hi = x.astype(BF16)
        return jnp.concatenate([hi, (x - hi.astype(F32)).astype(BF16)], axis=0)

    q = q_ref[...]
    q2 = split(q)
    q_lat = q2[:, :MLA_KV_RANK]
    q_pe = q2[:, MLA_KV_RANK:MLA_KV_RANK + MLA_ROPE]
    cs = [r[...].astype(BF16) for r in ckv_refs]
    ss = [lax.dot_general(q_lat, c, _NT, preferred_element_type=F32)
          + jnp.dot(q_pe, r[...].astype(BF16), preferred_element_type=F32)
          for c, r in zip(cs, kpe_refs)]
    s2 = jnp.concatenate(ss, axis=-1)
    s = s2[:hp] + s2[hp:]
    m_old = m_scr[...]
    m_new = jnp.maximum(m_old, jnp.max(s, -1, keepdims=True))
    alpha = jnp.exp(m_old - m_new)
    p = jnp.exp(s - m_new)
    l_scr[...] = alpha * l_scr[...] + jnp.sum(p, -1, keepdims=True)
    p2 = split(p)
    pv = [jnp.dot(p2[:, n * PAGE_SIZE:(n + 1) * PAGE_SIZE], c, preferred_element_type=F32)
          for n, c in enumerate(cs)]
    pv = functools.reduce(lambda a, b: a + b, pv)
    acc_scr[...] = alpha * acc_scr[...] + pv[:hp] + pv[hp:]
    m_scr[...] = m_new

    @pl.when(s_idx == pl.num_programs(1) - 1)
    def _():
        cn = cnew_ref[...]
        pn = pnew_ref[...]
        s_new = (jnp.sum(q[:, :MLA_KV_RANK] * cn, -1, keepdims=True)
                 + jnp.sum(q[:, MLA_KV_RANK:MLA_KV_RANK + MLA_ROPE] * pn, -1, keepdims=True))
        m_fin = jnp.maximum(m_scr[...], s_new)
        a_fin = jnp.exp(m_scr[...] - m_fin)
        p_new = jnp.exp(s_new - m_fin)
        l_fin = a_fin * l_scr[...] + p_new
        o_ref[...] = (a_fin * acc_scr[...] + p_new * cn) / l_fin


def _mla_sample_attend(layer, q_s, c_new, p_new, cache_ckv, cache_kpe_t, page_table, npg=32):
    dbsz, n_pages = page_table.shape
    hp = q_s.shape[1]

    def page_map(n):
        return lambda b, s, pt: (layer, pt[b, s * npg + n], 0, 0)

    grid_spec = pltpu.PrefetchScalarGridSpec(
        num_scalar_prefetch=1,
        grid=(dbsz, n_pages // npg),
        in_specs=[pl.BlockSpec((None, hp, MLA_QK), lambda b, s, pt: (b, 0, 0)),
                  pl.BlockSpec((None, 1, MLA_KV_RANK), lambda b, s, pt: (b, 0, 0)),
                  pl.BlockSpec((None, 1, MLA_ROPE), lambda b, s, pt: (b, 0, 0))]
                 + [pl.BlockSpec((None, None, PAGE_SIZE, MLA_KV_RANK), page_map(n)) for n in range(npg)]
                 + [pl.BlockSpec((None, None, MLA_ROPE, PAGE_SIZE), page_map(n)) for n in range(npg)],
        out_specs=pl.BlockSpec((None, hp, MLA_KV_RANK), lambda b, s, pt: (b, 0, 0)),
        scratch_shapes=[pltpu.VMEM((hp, 1), F32), pltpu.VMEM((hp, 1), F32), pltpu.VMEM((hp, MLA_KV_RANK), F32)],
    )
    return pl.pallas_call(
        functools.partial(_mla_sample_kernel, npg),
        grid_spec=grid_spec,
        out_shape=jax.ShapeDtypeStruct((dbsz, hp, MLA_KV_RANK), F32),
        compiler_params=_cp("parallel", "arbitrary"),
        name="mla_sample_attend",
    )(page_table, q_s, c_new, p_new, *([cache_ckv] * npg), *([cache_kpe_t] * npg))


def _mla_vup_kernel(o_ref, wv_ref, y_ref):
    ys = [_wdot(o_ref[h], wv_ref[h]) for h in range(MLA_HEADS)]
    y_ref[...] = jnp.concatenate(ys, axis=-1).astype(y_ref.dtype)


def _mla_vup(o_heads, wv):
    hp, dbsz, _ = o_heads.shape
    return pl.pallas_call(
        _mla_vup_kernel,
        grid=(1,),
        in_specs=[pl.BlockSpec((hp, dbsz, MLA_KV_RANK), lambda i: (0, 0, 0)),
                  pl.BlockSpec((MLA_HEADS, MLA_KV_RANK, MLA_V), lambda i: (0, 0, 0))],
        out_specs=pl.BlockSpec((dbsz, MLA_WIDTH), lambda i: (0, 0)),
        out_shape=jax.ShapeDtypeStruct((dbsz, MLA_WIDTH), F32),
        compiler_params=_cp("arbitrary"),
        name="mla_vup",
    )(o_heads, wv)


def _rope_tables(pos):
    half = MLA_ROPE // 2
    inv = ROPE_THETA ** (-jnp.arange(half, dtype=F32) / half)
    ang = pos.astype(F32)[:, None] * inv[None, :]
    cos = jnp.tile(jnp.cos(ang), (1, 2 * MLA_HEADS))
    sin = jnp.tile(jnp.sin(ang), (1, 2 * MLA_HEADS))
    return cos, sin


def _pad_lanes(v, width=LANES):
    return jnp.pad(v, (0, width - v.shape[0])).reshape(1, width)


def _rotate_half_cols(w):
    half = MLA_ROPE // 2
    return jnp.concatenate([-w[..., half:], w[..., :half]], axis=-1)


def kernel(x_prompt, x_sample, cache_ckv, cache_kpe, page_table, state_ssm, state_conv, state_wkv, state_shift,
           w_in, w_out, ln1_g, ln1_b, ln2_g, ln2_b,
           ssd_conv_w, ssd_conv_b, ssd_dt_bias, ssd_a_log, ssd_d, ssd_norm,
           mla_q_norm, mla_kv_norm, mla_q_up, mla_kv_up,
           rw_mu, rw_w0, rw_w_up, rw_a0, rw_a_up, rw_g_up, rw_k_k, rw_k_a, rw_r_k, rw_ln_g, rw_ln_b,
           ffn_gu, ffn_down, moe_router, moe_gu, moe_down):
    bsz, seqlen, d = x_prompt.shape
    dbsz = x_sample.shape[0]
    n_p = bsz * seqlen
    past_len = page_table.shape[1] * PAGE_SIZE
    xp = x_prompt.reshape(n_p, d)
    xs = x_sample.reshape(dbsz, d)
    tm = 512
    cos_p, sin_p = _rope_tables(jnp.tile(jnp.arange(seqlen), bsz))
    cos_s, sin_s = _rope_tables(jnp.full((dbsz,), past_len))
    cache_kpe_t = jnp.swapaxes(cache_kpe, 2, 3)
    row = lambda v: v.reshape(1, -1)
    lo = lambda w: w.astype(BF16)

    outs_p = [[] for _ in range(6)]
    outs_s = [[] for _ in range(6)]
    for l in range(DEPTH):
        wi = w_in[l]
        m0 = SSD_IN
        r0 = SSD_IN + MLA_IN
        w_ssd = jnp.pad(wi[:, :SSD_IN], ((0, 0), (0, SSD_UP - SSD_IN)))
        k_rope_w = wi[:, m0 + 1024:m0 + MLA_IN]
        w_mla = jnp.concatenate([wi[:, m0:m0 + MLA_IN], _rotate_half_cols(k_rope_w)], axis=1)
        w_rw = wi[:, r0:]
        q_up = mla_q_up[l]
        wqn = q_up[:, :, :MLA_NOPE].reshape(MLA_Q_RANK, -1)
        wqr = q_up[:, :, MLA_NOPE:].reshape(MLA_Q_RANK, -1)
        wqx = _rotate_half_cols(q_up[:, :, MLA_NOPE:]).reshape(MLA_Q_RANK, -1)
        kv_up = mla_kv_up[l]
        wkn = jnp.transpose(kv_up[:, :, :MLA_NOPE], (1, 2, 0))
        wv = jnp.transpose(kv_up[:, :, MLA_NOPE:], (1, 0, 2))
        mla_w = (wqn, wqr, wqx, wkn)
        mla_n = (row(mla_q_norm[l]), row(mla_kv_norm[l]))
        ssd_par = (ssd_conv_w[l], row(ssd_conv_b[l]), _pad_lanes(ssd_dt_bias[l]), _pad_lanes(ssd_a_log[l]))
        d_rep = row(jnp.repeat(ssd_d[l], HEAD_DIM))
        a_rep = row(jnp.repeat(-jnp.exp(ssd_a_log[l]), HEAD_DIM))
        rw_par_s = (row(rw_mu[l]), row(rw_w0[l]), rw_w_up[l], row(rw_a0[l]), rw_a_up[l], rw_g_up[l],
                    row(rw_k_k[l]), row(rw_k_a[l]))
        rw_par_p = tuple(lo(w) if w.shape[0] > 1 else w for w in rw_par_s)
        rw_post = (row(rw_r_k[l]), row(rw_ln_g[l]), row(rw_ln_b[l]))
        ln1 = (row(ln1_g[l]), row(ln1_b[l]))
        ln2 = (row(ln2_g[l]), row(ln2_b[l]))

        u_ssd = _matmul(xp, lo(w_ssd), tm)
        u_mla = _matmul(xp, lo(w_mla), tm)
        u_rw = _matmul(xp, lo(w_rw), tm)
        y_ssd_p, ssm_p = _ssd_prompt(u_ssd, bsz, seqlen, *ssd_par, d_rep, row(ssd_norm[l]))
        conv_p = u_ssd[:, SSD_WIDTH:SSD_WIDTH + SSD_XBC].reshape(bsz, seqlen, SSD_XBC)[:, seqlen - (SSD_CONV - 1):]
        ckv_p, kpe_p, q_p, kcat = _mla_proj(u_mla, cos_p, sin_p, *mla_n, *[lo(w) for w in mla_w], tm)
        y_mla_p = _mla_prompt_attend(q_p, kcat, lo(wv), bsz, seqlen)
        seq_p = _rw_prep_prompt(u_rw, n_p, seqlen, rw_par_p)
        y_rw_p, wkv_p = _rw_scan_prompt(seq_p, bsz, seqlen, *rw_post)
        shift_p = u_rw.reshape(bsz, seqlen, RWKV_IN)[:, -1]

        us_ssd = _matmul(xs, w_ssd, dbsz)
        us_mla = _matmul(xs, w_mla, dbsz)
        us_rw = _matmul(xs, w_rw, dbsz)
        y_ssd_s, ssm_s = _ssd_sample(l, us_ssd, 0, dbsz, state_conv[l].reshape(dbsz, -1), state_ssm,
                                     *ssd_par, a_rep, d_rep, row(ssd_norm[l]))
        conv_s = jnp.concatenate([state_conv[l][:, 1:], us_ssd[:, None, SSD_WIDTH:SSD_WIDTH + SSD_XBC]], axis=1)
        ckv_s, kpe_s, q_s, _ = _mla_proj(us_mla, cos_s, sin_s, *mla_n, *mla_w, dbsz)
        q_s = jnp.pad(jnp.transpose(q_s, (1, 0, 2)), ((0, 0), (0, 8 - MLA_HEADS), (0, 0)))
        o_lat = _mla_sample_attend(l, q_s, ckv_s[:, None, :], kpe_s[:, None, :], cache_ckv, cache_kpe_t, page_table)
        y_mla_s = _mla_vup(jnp.transpose(o_lat, (1, 0, 2)), wv)
        seq_s = _rw_prep_sample(us_rw, 0, dbsz, state_shift[l], rw_par_s)
        y_rw_s, wkv_s = _rw_step_sample(seq_s, state_wkv[l], dbsz, *rw_post)

        xp = _matmul_res_ln(jnp.concatenate([y_ssd_p, y_mla_p, y_rw_p], axis=-1), lo(w_out[l]), xp, *ln1, tm)
        xs = _matmul_res_ln(jnp.concatenate([y_ssd_s, y_mla_s, y_rw_s], axis=-1), w_out[l], xs, *ln1, dbsz)
        if l % 2 == 1:
            wr = jnp.pad(moe_router[l // 2], ((0, 0), (0, LANES - N_EXPERTS)))
            w_gu, w_dn = lo(moe_gu[l // 2]), lo(moe_down[l // 2])
            xp = _moe_dense(xp, _router(xp, wr, tm), w_gu, w_dn, *ln2, tm, 256)
            xs = _moe_dense(xs, _router(xs, wr, dbsz), w_gu, w_dn, *ln2, dbsz, 256)
        else:
            xp = _ffn_dense(xp, lo(ffn_gu[l // 2]), lo(ffn_down[l // 2]), *ln2, tm, 512)
            xs = _ffn_dense(xs, ffn_gu[l // 2], ffn_down[l // 2], *ln2, dbsz, 512)

        for lst, val in zip(outs_p, (ckv_p.reshape(bsz, seqlen, -1), kpe_p.reshape(bsz, seqlen, -1),
                                     ssm_p, conv_p, wkv_p, shift_p)):
            lst.append(val)
        for lst, val in zip(outs_s, (ckv_s.reshape(dbsz, 1, -1), kpe_s.reshape(dbsz, 1, -1),
                                     ssm_s, conv_s, wkv_s, us_rw)):
            lst.append(val)

    ckv_p, kpe_p, ssm_p, conv_p, wkv_p, shift_p = [jnp.stack(o) for o in outs_p]
    ckv_s, kpe_s, ssm_s, conv_s, wkv_s, shift_s = [jnp.stack(o) for o in outs_s]
    return (xp.reshape(bsz, seqlen, d), xs.reshape(dbsz, 1, d), ckv_p, kpe_p, ckv_s, kpe_s, ssm_p, ssm_s,
            conv_p, conv_s, wkv_p, wkv_s, shift_p, shift_s)
```

```python
import functools
import math

import jax
import jax.numpy as jnp
from jax import lax
from jax.experimental import pallas as pl
from jax.experimental.pallas import tpu as pltpu

F32 = jnp.float32
BF16 = jnp.bfloat16
HI = lax.Precision.HIGHEST

D_MODEL = 2048
PAGE_SIZE = 128
HEAD_DIM = 64
SSD_WIDTH = 768
SSD_HEADS = 12
SSD_GROUPS = 2
SSD_STATE = 128
SSD_CONV = 4
SSD_CHUNK = 128
SSD_XBC = 1280
SSD_IN = 2060
SSD_GW = SSD_WIDTH // SSD_GROUPS
SSD_UP = 2176
MLA_HEADS = 5
MLA_NOPE = 128
MLA_ROPE = 64
MLA_V = 128
MLA_WIDTH = 640
MLA_Q_RANK = 512
MLA_KV_RANK = 512
MLA_IN = 1088
MLA_UP = 1152
MLA_QK = 640
MLA_SCALE = (MLA_NOPE + MLA_ROPE) ** -0.5
ROPE_THETA = 10000.0
RWKV_WIDTH = 640
RWKV_HEADS = 10
RWKV_IN = 2176
RWKV_GN_EPS = 64e-5
RWKV_CHUNK = 64
RWKV_SUB = 16
FFN_DENSE = 5632
N_EXPERTS = 8
FFN_EXPERT = 2816
MOE_CAP = 192
DEPTH = 2
ALPHA = (2.0 * DEPTH) ** 0.25
LN_EPS = 1e-5
RMS_EPS = 1e-6
LANES = 128
NEG_BIG = -1e30

VMEM_LIMIT = 56 * 1024 * 1024


def _cp(*sem):
    return pltpu.CompilerParams(dimension_semantics=sem, vmem_limit_bytes=VMEM_LIMIT)


def _dot(a, b):
    return jnp.dot(a.astype(BF16), b.astype(BF16), preferred_element_type=F32)


def _dot_nt(a, b):
    return lax.dot_general(a.astype(BF16), b.astype(BF16), (((1,), (1,)), ((), ())), preferred_element_type=F32)


def _dot_tn(a, b):
    return lax.dot_general(a.astype(BF16), b.astype(BF16), (((0,), (0,)), ((), ())), preferred_element_type=F32)


def _dot_hi(a, b):
    return jnp.dot(a, b, precision=HI, preferred_element_type=F32)


_NN = (((1,), (0,)), ((), ()))
_NT = (((1,), (1,)), ((), ()))
_TN = (((0,), (0,)), ((), ()))


def _dot3(a, b, dims):
    ah = a.astype(BF16)
    bh = b.astype(BF16)
    al = (a - ah.astype(F32)).astype(BF16)
    bl = (b - bh.astype(F32)).astype(BF16)
    out = lax.dot_general(ah, bh, dims, preferred_element_type=F32)
    out = out + lax.dot_general(ah, bl, dims, preferred_element_type=F32)
    return out + lax.dot_general(al, bh, dims, preferred_element_type=F32)


def _sigmoid(x):
    return 1.0 / (1.0 + jnp.exp(-x))


def _silu(x):
    return x * _sigmoid(x)


def _softplus(x):
    return jnp.maximum(x, 0.0) + jnp.log1p(jnp.exp(-jnp.abs(x)))


def _iota(shape, dim):
    return lax.broadcasted_iota(jnp.int32, shape, dim)


def _seg_sum(x, width, seg):
    same = (_iota((width, width), 0) // seg) == (_iota((width, width), 1) // seg)
    ones = jnp.where(same, 1.0, 0.0).astype(BF16)
    hi = x.astype(BF16)
    lo = (x - hi.astype(F32)).astype(BF16)
    return (jnp.dot(hi, ones, preferred_element_type=F32) + jnp.dot(lo, ones, preferred_element_type=F32))


def _layer_norm(x, g, b):
    mu = jnp.mean(x, -1, keepdims=True)
    xc = x - mu
    var = jnp.mean(xc * xc, -1, keepdims=True)
    return xc * lax.rsqrt(var + LN_EPS) * g + b


def _rms_norm(x, g):
    return x * lax.rsqrt(jnp.mean(x * x, -1, keepdims=True) + RMS_EPS) * g


def _wdot(x, w):
    if w.dtype == BF16:
        return jnp.dot(x.astype(BF16), w, preferred_element_type=F32)
    return jnp.dot(x.astype(F32), w, precision=HI, preferred_element_type=F32)


def _row_tile(rows, pref):
    return pref if rows % pref == 0 else rows


def _mm_kernel(x_ref, w_ref, o_ref):
    o_ref[...] = _wdot(x_ref[...], w_ref[...]).astype(o_ref.dtype)


def _matmul(x, w, tm, out_dtype=F32):
    m, k = x.shape
    n = w.shape[1]
    tm = _row_tile(m, tm)
    return pl.pallas_call(
        _mm_kernel,
        grid=(m // tm,),
        in_specs=[pl.BlockSpec((tm, k), lambda i: (i, 0)), pl.BlockSpec((k, n), lambda i: (0, 0))],
        out_specs=pl.BlockSpec((tm, n), lambda i: (i, 0)),
        out_shape=jax.ShapeDtypeStruct((m, n), out_dtype),
        compiler_params=_cp("parallel"),
        name="matmul",
    )(x, w)


def _mm_res_ln_kernel(a_ref, w_ref, res_ref, g_ref, b_ref, o_ref):
    mix = _wdot(a_ref[...], w_ref[...])
    o_ref[...] = _layer_norm(ALPHA * res_ref[...] + mix, g_ref[...], b_ref[...])


def _matmul_res_ln(a, w, res, g, b, tm):
    m, k = a.shape
    n = w.shape[1]
    tm = _row_tile(m, tm)
    return pl.pallas_call(
        _mm_res_ln_kernel,
        grid=(m // tm,),
        in_specs=[pl.BlockSpec((tm, k), lambda i: (i, 0)), pl.BlockSpec((k, n), lambda i: (0, 0)),
                  pl.BlockSpec((tm, n), lambda i: (i, 0)), pl.BlockSpec((1, n), lambda i: (0, 0)),
                  pl.BlockSpec((1, n), lambda i: (0, 0))],
        out_specs=pl.BlockSpec((tm, n), lambda i: (i, 0)),
        out_shape=jax.ShapeDtypeStruct((m, n), F32),
        compiler_params=_cp("parallel"),
        name="out_proj_ln",
    )(a, w, res, g, b)


def _ffn_kernel(x_ref, wg_ref, wu_ref, wd_ref, g_ref, b_ref, o_ref, xb_scr):
    f = pl.program_id(1)

    @pl.when(f == 0)
    def _():
        xb_scr[...] = x_ref[...].astype(xb_scr.dtype)
        o_ref[...] = ALPHA * x_ref[...]

    xb = xb_scr[...]
    act = _silu(_wdot(xb, wg_ref[...])) * _wdot(xb, wu_ref[...])
    o_ref[...] += _wdot(act, wd_ref[...])

    @pl.when(f == pl.num_programs(1) - 1)
    def _():
        o_ref[...] = _layer_norm(o_ref[...], g_ref[...], b_ref[...])


def _ffn_dense(x, w_gu, w_down, g, b, tm, tf):
    m, d = x.shape
    fdim = w_down.shape[0]
    nf = fdim // tf
    tm = _row_tile(m, tm)
    return pl.pallas_call(
        _ffn_kernel,
        grid=(m // tm, nf),
        in_specs=[pl.BlockSpec((tm, d), lambda i, f: (i, 0)),
                  pl.BlockSpec((d, tf), lambda i, f: (0, f)),
                  pl.BlockSpec((d, tf), lambda i, f: (0, nf + f)),
                  pl.BlockSpec((tf, d), lambda i, f: (f, 0)),
                  pl.BlockSpec((1, d), lambda i, f: (0, 0)),
                  pl.BlockSpec((1, d), lambda i, f: (0, 0))],
        out_specs=pl.BlockSpec((tm, d), lambda i, f: (i, 0)),
        out_shape=jax.ShapeDtypeStruct((m, d), F32),
        scratch_shapes=[pltpu.VMEM((tm, d), w_gu.dtype)],
        compiler_params=_cp("parallel", "arbitrary"),
        name="ffn_dense",
    )(x, w_gu, w_gu, w_down, g, b)


def _router_kernel(x_ref, wr_ref, gates_ref):
    logits = _dot_hi(x_ref[...], wr_ref[...])
    lane = _iota(logits.shape, 1)
    valid = lane < N_EXPERTS
    logits = jnp.where(valid, logits, NEG_BIG)
    mx = jnp.max(logits, -1, keepdims=True)
    ex = jnp.where(valid, jnp.exp(logits - mx), 0.0)
    probs = ex / jnp.sum(ex, -1, keepdims=True)
    p1 = jnp.max(probs, -1, keepdims=True)
    i1 = jnp.min(jnp.where(probs == p1, lane, LANES), -1, keepdims=True)
    first = lane == i1
    rest = jnp.where(first | (~valid), -1.0, probs)
    p2 = jnp.max(rest, -1, keepdims=True)
    i2 = jnp.min(jnp.where(rest == p2, lane, LANES), -1, keepdims=True)
    second = lane == i2
    tot = p1 + p2
    gates_ref[...] = jnp.where(first, p1 / tot, 0.0) + jnp.where(second, p2 / tot, 0.0)


def _router(x, wr_pad, tm):
    m, d = x.shape
    tm = _row_tile(m, tm)
    return pl.pallas_call(
        _router_kernel,
        grid=(m // tm,),
        in_specs=[pl.BlockSpec((tm, d), lambda i: (i, 0)), pl.BlockSpec((d, LANES), lambda i: (0, 0))],
        out_specs=pl.BlockSpec((tm, LANES), lambda i: (i, 0)),
        out_shape=jax.ShapeDtypeStruct((m, LANES), F32),
        compiler_params=_cp("parallel"),
        name="moe_router",
    )(x, wr_pad)


def _moe_kernel(x_ref, gates_ref, wg_ref, wu_ref, wd_ref, g_ref, b_ref, o_ref, xb_scr):
    e = pl.program_id(1)
    f = pl.program_id(2)

    @pl.when((e == 0) & (f == 0))
    def _():
        xb_scr[...] = x_ref[...].astype(BF16)
        o_ref[...] = ALPHA * x_ref[...]

    gates = gates_ref[...]
    gate = jnp.sum(jnp.where(_iota(gates.shape, 1) == e, gates, 0.0), -1, keepdims=True)
    xb = xb_scr[...]
    hg = jnp.dot(xb, wg_ref[...], preferred_element_type=F32)
    hu = jnp.dot(xb, wu_ref[...], preferred_element_type=F32)
    act = (_silu(hg) * hu * gate).astype(BF16)
    o_ref[...] += jnp.dot(act, wd_ref[...], preferred_element_type=F32)

    @pl.when((e == pl.num_programs(1) - 1) & (f == pl.num_programs(2) - 1))
    def _():
        o_ref[...] = _layer_norm(o_ref[...], g_ref[...], b_ref[...])


def _moe_dense(x, gates, w_gu, w_down, g, b, tm, tf):
    m, d = x.shape
    ne, fdim, _ = w_down.shape
    tm = _row_tile(m, tm)
    nf = fdim // tf
    return pl.pallas_call(
        _moe_kernel,
        grid=(m // tm, ne, nf),
        in_specs=[pl.BlockSpec((tm, d), lambda i, e, f: (i, 0)),
                  pl.BlockSpec((tm, LANES), lambda i, e, f: (i, 0)),
                  pl.BlockSpec((None, d, tf), lambda i, e, f: (e, 0, f)),
                  pl.BlockSpec((None, d, tf), lambda i, e, f: (e, 0, nf + f)),
                  pl.BlockSpec((None, tf, d), lambda i, e, f: (e, f, 0)),
                  pl.BlockSpec((1, d), lambda i, e, f: (0, 0)),
                  pl.BlockSpec((1, d), lambda i, e, f: (0, 0))],
        out_specs=pl.BlockSpec((tm, d), lambda i, e, f: (i, 0)),
        out_shape=jax.ShapeDtypeStruct((m, d), F32),
        scratch_shapes=[pltpu.VMEM((tm, d), BF16)],
        compiler_params=_cp("parallel", "arbitrary", "arbitrary"),
        name="moe_dense",
    )(x, gates, w_gu, w_gu, w_down, g, b)


def _moe_compact_kernel(x_ref, rank_ref, gate_ref, wg_ref, wu_ref, wd_ref, g_ref, b_ref, o_ref,
                        xb_scr, sel_scr, xg_scr, gc_scr, acc_scr):
    e = pl.program_id(1)
    f = pl.program_id(2)
    cap, tm = sel_scr.shape

    @pl.when((e == 0) & (f == 0))
    def _():
        xb_scr[...] = x_ref[...].astype(BF16)
        o_ref[...] = ALPHA * x_ref[...]

    @pl.when(f == 0)
    def _():
        rank = rank_ref[pl.ds(e, 1), :]
        hit = _iota((cap, tm), 0).astype(F32) == rank
        sel = jnp.where(hit, 1.0, 0.0).astype(BF16)
        sel_scr[...] = sel
        xg_scr[...] = jnp.dot(sel, xb_scr[...], preferred_element_type=F32).astype(BF16)
        gate = jnp.sum(jnp.where(hit, gate_ref[pl.ds(e, 1), :], 0.0), -1, keepdims=True)
        gc_scr[...] = jnp.broadcast_to(gate, gc_scr.shape)
        acc_scr[...] = jnp.zeros_like(acc_scr)

    xg = xg_scr[...]
    hg = jnp.dot(xg, wg_ref[...], preferred_element_type=F32)
    hu = jnp.dot(xg, wu_ref[...], preferred_element_type=F32)
    act = (_silu(hg) * hu * gc_scr[:, 0:1]).astype(BF16)
    acc_scr[...] += jnp.dot(act, wd_ref[...], preferred_element_type=F32)

    @pl.when(f == pl.num_programs(2) - 1)
    def _():
        acc = acc_scr[...]
        hi = acc.astype(BF16)
        lo = (acc - hi.astype(F32)).astype(BF16)
        sel = sel_scr[...]
        o_ref[...] += (lax.dot_general(sel, hi, _TN, preferred_element_type=F32)
                       + lax.dot_general(sel, lo, _TN, preferred_element_type=F32))

    @pl.when((e == pl.num_programs(1) - 1) & (f == pl.num_programs(2) - 1))
    def _():
        o_ref[...] = _layer_norm(o_ref[...], g_ref[...], b_ref[...])


def _moe_compact(x, rank_t, gate_t, w_gu, w_down, g, b, tm, tf, cap):
    m, d = x.shape
    ne, fdim, _ = w_down.shape
    nf = fdim // tf
    return pl.pallas_call(
        _moe_compact_kernel,
        grid=(m // tm, ne, nf),
        in_specs=[pl.BlockSpec((tm, d), lambda i, e, f: (i, 0)),
                  pl.BlockSpec((None, ne, tm), lambda i, e, f: (i, 0, 0)),
                  pl.BlockSpec((None, ne, tm), lambda i, e, f: (i, 0, 0)),
                  pl.BlockSpec((None, d, tf), lambda i, e, f: (e, 0, f)),
                  pl.BlockSpec((None, d, tf), lambda i, e, f: (e, 0, nf + f)),
                  pl.BlockSpec((None, tf, d), lambda i, e, f: (e, f, 0)),
                  pl.BlockSpec((1, d), lambda i, e, f: (0, 0)),
                  pl.BlockSpec((1, d), lambda i, e, f: (0, 0))],
        out_specs=pl.BlockSpec((tm, d), lambda i, e, f: (i, 0)),
        out_shape=jax.ShapeDtypeStruct((m, d), F32),
        scratch_shapes=[pltpu.VMEM((tm, d), BF16), pltpu.VMEM((cap, tm), BF16), pltpu.VMEM((cap, d), BF16),
                        pltpu.VMEM((cap, LANES), F32), pltpu.VMEM((cap, d), F32)],
        compiler_params=_cp("parallel", "arbitrary", "arbitrary"),
        name="moe_compact",
    )(x, rank_t, gate_t, w_gu, w_gu, w_down, g, b)


def _moe_prompt(x, gates, w_gu, w_down, g, b, tm, tf, cap):
    m = x.shape[0]
    if m % tm != 0 or cap >= tm:
        return _moe_dense(x, gates, w_gu, w_down, g, b, tm, tf)
    nt = m // tm
    gate_e = gates[:, :N_EXPERTS].reshape(nt, tm, N_EXPERTS)
    routed = gate_e > 0.0
    slot = jnp.cumsum(routed.astype(jnp.int32), axis=1) - 1
    rank_t = jnp.swapaxes(jnp.where(routed, slot, -1).astype(F32), 1, 2)
    gate_t = jnp.swapaxes(gate_e, 1, 2)
    fits = jnp.max(jnp.sum(routed.astype(jnp.int32), axis=1)) <= cap
    return lax.cond(fits,
                    lambda: _moe_compact(x, rank_t, gate_t, w_gu, w_down, g, b, tm, tf, cap),
                    lambda: _moe_dense(x, gates, w_gu, w_down, g, b, tm, tf))


def _ssd_gate_norm(y, xs, z, d_rep, norm_w):
    y = (y + d_rep * xs) * _silu(z)
    outs = []
    for g in range(SSD_GROUPS):
        sl = slice(g * SSD_GW, (g + 1) * SSD_GW)
        outs.append(_rms_norm(y[:, sl], norm_w[:, sl]))
    return jnp.concatenate(outs, axis=-1)


def _ssd_prompt_kernel(u_ref, cw_ref, cb_ref, dtb_ref, alog_ref, drep_ref, nw_ref,
                       y_ref, hout_ref, xf_scr, h_scr):
    q = SSD_CHUNK
    c = pl.program_id(1)

    @pl.when(c == 0)
    def _():
        xf_scr[0:8, :] = jnp.zeros((8, SSD_XBC), F32)
        h_scr[...] = jnp.zeros_like(h_scr)

    xf_scr[8:8 + q, :] = u_ref[:, SSD_WIDTH:SSD_WIDTH + SSD_XBC]
    acc = cb_ref[...] + xf_scr[5:5 + q, :] * cw_ref[0:1, :]
    for k in range(1, SSD_CONV):
        acc = acc + xf_scr[5 + k:5 + k + q, :] * cw_ref[k:k + 1, :]
    xf_scr[5:8, :] = xf_scr[5 + q:8 + q, :]
    xbc = _silu(acc)
    xs = xbc[:, :SSD_WIDTH]
    bmat = xbc[:, SSD_WIDTH:SSD_WIDTH + SSD_GROUPS * SSD_STATE]
    cmat = xbc[:, SSD_WIDTH + SSD_GROUPS * SSD_STATE:]

    dt = _softplus(u_ref[:, SSD_WIDTH + SSD_XBC:] + dtb_ref[...])
    da = dt * (-jnp.exp(alog_ref[...]))
    row = _iota((q, q), 0)
    col = _iota((q, q), 1)
    causal = row >= col
    acum = _dot_hi(jnp.where(causal, 1.0, 0.0), da)
    dt_t = dt.T
    acum_t = _dot_hi(da.T, jnp.where(row <= col, 1.0, 0.0))

    scores = [_dot_nt(cmat[:, g * SSD_STATE:(g + 1) * SSD_STATE], bmat[:, g * SSD_STATE:(g + 1) * SSD_STATE])
              for g in range(SSD_GROUPS)]
    rep = SSD_HEADS // SSD_GROUPS
    heads = range(SSD_HEADS)
    gsl = [slice((h // rep) * SSD_STATE, (h // rep + 1) * SSD_STATE) for h in heads]
    a_col = [acum[:, h:h + 1] for h in heads]
    a_last = [acum[q - 1:q, h:h + 1] for h in heads]
    xs_h = [xs[:, h * HEAD_DIM:(h + 1) * HEAD_DIM] for h in heads]
    h_prev = [h_scr[h] for h in heads]
    m = [scores[h // rep] * jnp.exp(jnp.where(causal, a_col[h] - acum_t[h:h + 1, :], NEG_BIG)) * dt_t[h:h + 1, :]
         for h in heads]
    y_diag = [_dot(m[h], xs_h[h]) for h in heads]
    y_off = [jnp.exp(a_col[h]) * _dot_nt(cmat[:, gsl[h]], h_prev[h]) for h in heads]
    xw = [xs_h[h] * (jnp.exp(a_last[h] - a_col[h]) * dt[:, h:h + 1]) for h in heads]
    h_new = [jnp.exp(a_last[h]) * h_prev[h] + _dot_tn(xw[h], bmat[:, gsl[h]]) for h in heads]
    for h in heads:
        h_scr[h] = h_new[h]
    y_all = jnp.concatenate([y_diag[h] + y_off[h] for h in heads], axis=-1)

    y_ref[...] = _ssd_gate_norm(y_all, xs, u_ref[:, :SSD_WIDTH], drep_ref[...], nw_ref[...]).astype(y_ref.dtype)

    @pl.when(c == pl.num_programs(1) - 1)
    def _():
        hout_ref[...] = h_scr[...]


def _ssd_prompt(u_ssd, bsz, seqlen, cw, cb, dtb, alog, drep, nw):
    q = SSD_CHUNK
    nc = seqlen // q
    full = lambda shape: pl.BlockSpec(shape, lambda b, c: (0,) * len(shape))
    return pl.pallas_call(
        _ssd_prompt_kernel,
        grid=(bsz, nc),
        in_specs=[pl.BlockSpec((q, SSD_UP), lambda b, c: (b * nc + c, 0)),
                  full((SSD_CONV, SSD_XBC)), full((1, SSD_XBC)), full((1, LANES)), full((1, LANES)),
                  full((1, SSD_WIDTH)), full((1, SSD_WIDTH))],
        out_specs=[pl.BlockSpec((q, SSD_WIDTH), lambda b, c: (b * nc + c, 0)),
                   pl.BlockSpec((None, SSD_HEADS, HEAD_DIM, SSD_STATE), lambda b, c: (b, 0, 0, 0))],
        out_shape=[jax.ShapeDtypeStruct((bsz * seqlen, SSD_WIDTH), BF16),
                   jax.ShapeDtypeStruct((bsz, SSD_HEADS, HEAD_DIM, SSD_STATE), F32)],
        scratch_shapes=[pltpu.VMEM((8 + q, SSD_XBC), F32),
                        pltpu.VMEM((SSD_HEADS, HEAD_DIM, SSD_STATE), F32)],
        compiler_params=_cp("parallel", "arbitrary"),
        name="ssd_prompt",
    )(u_ssd, cw, cb, dtb, alog, drep, nw)


def _ssd_sample_kernel(u_ref, cbuf_ref, h0_ref, cw_ref, cb_ref, dtb_ref, alog_ref, arep_ref, drep_ref, nw_ref,
                       y_ref, hout_ref):
    bt = u_ref.shape[0]
    x_new = u_ref[:, SSD_WIDTH:SSD_WIDTH + SSD_XBC]
    acc = cb_ref[...] + x_new * cw_ref[SSD_CONV - 1:SSD_CONV, :]
    for k in range(SSD_CONV - 1):
        acc = acc + cbuf_ref[:, k * SSD_XBC:(k + 1) * SSD_XBC] * cw_ref[k:k + 1, :]
    xbc = _silu(acc)
    xs = xbc[:, :SSD_WIDTH]
    bmat = xbc[:, SSD_WIDTH:SSD_WIDTH + SSD_GROUPS * SSD_STATE]
    cmat = xbc[:, SSD_WIDTH + SSD_GROUPS * SSD_STATE:]
    dt = _softplus(u_ref[:, SSD_WIDTH + SSD_XBC:] + dtb_ref[...])
    expand = jnp.where(_iota((LANES, SSD_WIDTH), 0) == _iota((LANES, SSD_WIDTH), 1) // HEAD_DIM, 1.0, 0.0)
    dt_rep = _dot_hi(dt, expand)
    dec_rep = jnp.exp(dt_rep * arep_ref[...])
    dtx = dt_rep * xs
    rows = _iota((bt, 1), 0)
    ones = jnp.ones((bt, SSD_STATE), F32)
    rep = SSD_HEADS // SSD_GROUPS
    seqs = range(bt)
    sel = [rows == bi for bi in seqs]
    y_cols = []
    for g in range(SSD_GROUPS):
        cs = slice(g * SSD_GW, (g + 1) * SSD_GW)
        gs = slice(g * SSD_STATE, (g + 1) * SSD_STATE)
        hsl = slice(g * rep, (g + 1) * rep)
        h0 = [h0_ref[bi, hsl].reshape(SSD_GW, SSD_STATE) for bi in seqs]
        outer = [_dot3(jnp.where(sel[bi], dtx[:, cs], 0.0), bmat[:, gs], _TN) for bi in seqs]
        decm = [_dot3(jnp.where(sel[bi], dec_rep[:, cs], 0.0), ones, _TN) for bi in seqs]
        hn = [decm[bi] * h0[bi] + outer[bi] for bi in seqs]
        for bi in seqs:
            hout_ref[bi, hsl] = hn[bi].reshape(rep, HEAD_DIM, SSD_STATE)
        ys = [_dot3(jnp.where(sel[bi], cmat[:, gs], 0.0), hn[bi], _NT) for bi in seqs]
        y_cols.append(functools.reduce(lambda a, b: a + b, ys))
    y_all = jnp.concatenate(y_cols, axis=-1)
    y_ref[...] = _ssd_gate_norm(y_all, xs, u_ref[:, :SSD_WIDTH], drep_ref[...], nw_ref[...]).astype(y_ref.dtype)


def _ssd_sample(layer, u_ssd, row0, dbsz, conv_flat, h0, cw, cb, dtb, alog, arep, drep, nw, bt=8):
    full = lambda shape: pl.BlockSpec(shape, lambda i: (0,) * len(shape))
    blk0 = row0 // bt
    return pl.pallas_call(
        _ssd_sample_kernel,
        grid=(dbsz // bt,),
        in_specs=[pl.BlockSpec((bt, SSD_UP), lambda i: (blk0 + i, 0)),
                  pl.BlockSpec((bt, (SSD_CONV - 1) * SSD_XBC), lambda i: (i, 0)),
                  pl.BlockSpec((None, bt, SSD_HEADS, HEAD_DIM, SSD_STATE), lambda i: (layer, i, 0, 0, 0)),
                  full((SSD_CONV, SSD_XBC)), full((1, SSD_XBC)), full((1, LANES)), full((1, LANES)),
                  full((1, SSD_WIDTH)), full((1, SSD_WIDTH)), full((1, SSD_WIDTH))],
        out_specs=[pl.BlockSpec((bt, SSD_WIDTH), lambda i: (i, 0)),
                   pl.BlockSpec((bt, SSD_HEADS, HEAD_DIM, SSD_STATE), lambda i: (i, 0, 0, 0))],
        out_shape=[jax.ShapeDtypeStruct((dbsz, SSD_WIDTH), F32),
                   jax.ShapeDtypeStruct((dbsz, SSD_HEADS, HEAD_DIM, SSD_STATE), F32)],
        compiler_params=_cp("parallel"),
        name="ssd_sample",
    )(u_ssd, conv_flat, h0, cw, cb, dtb, alog, arep, drep, nw)


def _rw_prep_body(u, prev, mu, w0, w_up, a0, a_up, g_up, k_k, k_a):
    o1, o2, o3 = RWKV_WIDTH, 2 * RWKV_WIDTH, 3 * RWKV_WIDTH
    o4, o5 = o3 + 64, o3 + 128
    ux = u + (prev - u) * mu
    r, k, v = ux[:, :o1], ux[:, o1:o2], ux[:, o2:o3]
    xw, xa, xg = ux[:, o3:o4], ux[:, o4:o5], ux[:, o5:]
    w = -_softplus(-(w0 + _wdot(jnp.tanh(xw), w_up))) - 0.5
    lw = -jnp.exp(w)
    a = _sigmoid(a0 + _wdot(xa, a_up))
    g = _wdot(_sigmoid(xg), g_up)
    kk = k * k_k
    nrm = jnp.maximum(jnp.sqrt(_seg_sum(kk * kk, RWKV_WIDTH, HEAD_DIM)), 1e-12)
    kap = kk / nrm
    k = k * (1.0 + (a - 1.0) * k_a)
    return r, lw, k, v, kap, kap * a, g


def _rw_prep_prompt_kernel(tiles_per_seq, u_ref, mu_ref, w0_ref, wup_ref, a0_ref, aup_ref, gup_ref, kk_ref, ka_ref,
                           r_ref, lw_ref, k_ref, v_ref, kap_ref, beta_ref, g_ref, scr):
    tm = u_ref.shape[0]
    i = pl.program_id(0)

    @pl.when(i % tiles_per_seq == 0)
    def _():
        scr[0:8, :] = jnp.zeros((8, RWKV_IN), F32)

    scr[8:8 + tm, :] = u_ref[...]
    prev = scr[7:7 + tm, :]
    outs = _rw_prep_body(u_ref[...], prev, mu_ref[...], w0_ref[...], wup_ref[...], a0_ref[...], aup_ref[...],
                         gup_ref[...], kk_ref[...], ka_ref[...])
    scr[7:8, :] = scr[7 + tm:8 + tm, :]
    for ref, val in zip((r_ref, lw_ref, k_ref, v_ref, kap_ref, beta_ref, g_ref), outs):
        ref[...] = val


def _rw_prep_sample_kernel(u_ref, prev_ref, mu_ref, w0_ref, wup_ref, a0_ref, aup_ref, gup_ref, kk_ref, ka_ref,
                           r_ref, lw_ref, k_ref, v_ref, kap_ref, beta_ref, g_ref):
    outs = _rw_prep_body(u_ref[...], prev_ref[...], mu_ref[...], w0_ref[...], wup_ref[...], a0_ref[...],
                         aup_ref[...], gup_ref[...], kk_ref[...], ka_ref[...])
    for ref, val in zip((r_ref, lw_ref, k_ref, v_ref, kap_ref, beta_ref, g_ref), outs):
        ref[...] = val


def _rw_param_specs(nargs):
    full = lambda shape: pl.BlockSpec(shape, lambda *a: (0,) * len(shape))
    del nargs
    return [full((1, RWKV_IN)), full((1, RWKV_WIDTH)), full((64, RWKV_WIDTH)), full((1, RWKV_WIDTH)),
            full((64, RWKV_WIDTH)), full((128, RWKV_WIDTH)), full((1, RWKV_WIDTH)), full((1, RWKV_WIDTH))]


def _rw_prep_prompt(u_rw, nrows, seqlen, params, tm=256):
    outs = [jax.ShapeDtypeStruct((nrows, RWKV_WIDTH), F32)] * 7
    return pl.pallas_call(
        functools.partial(_rw_prep_prompt_kernel, seqlen // tm),
        grid=(nrows // tm,),
        in_specs=[pl.BlockSpec((tm, RWKV_IN), lambda i: (i, 0))] + _rw_param_specs(1),
        out_specs=[pl.BlockSpec((tm, RWKV_WIDTH), lambda i: (i, 0))] * 7,
        out_shape=outs,
        scratch_shapes=[pltpu.VMEM((8 + tm, RWKV_IN), F32)],
        compiler_params=_cp("arbitrary"),
        name="rwkv_prep_prompt",
    )(u_rw, *params)


def _rw_prep_sample(u_rw, row0, dbsz, prev, params):
    outs = [jax.ShapeDtypeStruct((dbsz, RWKV_WIDTH), F32)] * 7
    return pl.pallas_call(
        _rw_prep_sample_kernel,
        grid=(1,),
        in_specs=[pl.BlockSpec((dbsz, RWKV_IN), lambda i: (row0 // dbsz, 0)),
                  pl.BlockSpec((dbsz, RWKV_IN), lambda i: (0, 0))] + _rw_param_specs(1),
        out_specs=[pl.BlockSpec((dbsz, RWKV_WIDTH), lambda i: (0, 0))] * 7,
        out_shape=outs,
        compiler_params=_cp("arbitrary"),
        name="rwkv_prep_sample",
    )(u_rw, prev, *params)


def _rw_post(y, r, k, v, g, rk, ln_g, ln_b):
    mu = _seg_sum(y, RWKV_WIDTH, HEAD_DIM) * (1.0 / HEAD_DIM)
    yc = y - mu
    var = _seg_sum(yc * yc, RWKV_WIDTH, HEAD_DIM) * (1.0 / HEAD_DIM)
    yn = yc * lax.rsqrt(var + RWKV_GN_EPS) * ln_g + ln_b
    bonus = _seg_sum(r * k * rk, RWKV_WIDTH, HEAD_DIM) * v
    return (yn + bonus) * g


def _unit_lower_inverse(a, c, sub):
    row = _iota((c, c), 0)
    col = _iota((c, c), 1)
    eye = jnp.where(row == col, 1.0, 0.0)
    blk = (row // sub) == (col // sub)
    d = [jnp.where(blk, m, 0.0) for m in a]
    n = [m - dm for m, dm in zip(a, d)]
    x = [eye - dm for dm in d]
    p = d
    for _ in range(int(math.log2(sub)) - 1):
        p = [_dot3(m, m, _NN) for m in p]
        x = [xm + _dot3(xm, pm, _NN) for xm, pm in zip(x, p)]
    e = [_dot3(xm, nm, _NN) for xm, nm in zip(x, n)]
    y = [eye - em for em in e]
    p = e
    for _ in range(int(math.log2(c // sub)) - 1):
        p = [_dot3(m, m, _NN) for m in p]
        y = [ym + _dot3(ym, pm, _NN) for ym, pm in zip(y, p)]
    return [_dot3(ym, xm, _NN) for ym, xm in zip(y, x)]


def _rw_scan_kernel(r_ref, lw_ref, k_ref, v_ref, kap_ref, beta_ref, g_ref, rk_ref, lng_ref, lnb_ref,
                    y_ref, sout_ref, s_scr, y_scr):
    c = RWKV_CHUNK
    ci = pl.program_id(1)

    @pl.when(ci == 0)
    def _():
        s_scr[...] = jnp.zeros_like(s_scr)

    r, lw, k, v = r_ref[...], lw_ref[...], k_ref[...], v_ref[...]
    tri = jnp.where(_iota((c, c), 0) >= _iota((c, c), 1), 1.0, 0.0)
    logp = _dot_hi(tri, lw)
    p_inv = jnp.exp(-logp)
    p_end = jnp.exp(logp[c - 1:c, :])
    kr = jnp.concatenate([kap_ref[...] * jnp.exp(logp - lw), r * jnp.exp(logp)], axis=0)
    kb = k * p_inv
    bb = beta_ref[...] * p_inv
    hat = jnp.concatenate([kb * p_end, bb * p_end], axis=0)
    row2 = _iota((2 * c, c), 0)
    col2 = _iota((2 * c, c), 1)
    keep = col2 < jnp.where(row2 < c, row2, row2 - c + 1)
    heads = range(RWKV_HEADS)
    hsl = [slice(h * HEAD_DIM, (h + 1) * HEAD_DIM) for h in heads]
    kr_h = [kr[:, s] for s in hsl]
    v_h = [v[:, s] for s in hsl]
    s0 = [s_scr[h] for h in heads]
    g_b = [jnp.where(keep, _dot3(kr_h[h], bb[:, hsl[h]], _NT), 0.0) for h in heads]
    g_k = [jnp.where(keep, _dot3(kr_h[h], kb[:, hsl[h]], _NT), 0.0) for h in heads]
    tinv = _unit_lower_inverse([m[:c] for m in g_b], c, RWKV_SUB)
    z = [_dot3(kr_h[h], s0[h], _NT) + _dot3(g_k[h], v_h[h], _NN) for h in heads]
    u_m = [_dot3(tinv[h], z[h][:c], _NN) for h in heads]
    for h in heads:
        y_scr[:, hsl[h]] = z[h][c:] - _dot3(g_b[h][c:], u_m[h], _NN)
    for h in heads:
        vu = jnp.concatenate([v_h[h], -u_m[h]], axis=0)
        s_scr[h] = s0[h] * p_end[:, hsl[h]] + _dot3(vu, hat[:, hsl[h]], _TN)

    y_ref[...] = _rw_post(y_scr[...], r, k, v, g_ref[...], rk_ref[...], lng_ref[...], lnb_ref[...]).astype(y_ref.dtype)

    @pl.when(ci == pl.num_programs(1) - 1)
    def _():
        sout_ref[...] = s_scr[...]


def _rw_scan_prompt(seqs, bsz, seqlen, rk, ln_g, ln_b):
    c = RWKV_CHUNK
    nc = seqlen // c
    tok = pl.BlockSpec((c, RWKV_WIDTH), lambda b, i: (b * nc + i, 0))
    par = pl.BlockSpec((1, RWKV_WIDTH), lambda b, i: (0, 0))
    return pl.pallas_call(
        _rw_scan_kernel,
        grid=(bsz, nc),
        in_specs=[tok] * 7 + [par] * 3,
        out_specs=[tok, pl.BlockSpec((None, RWKV_HEADS, HEAD_DIM, HEAD_DIM), lambda b, i: (b, 0, 0, 0))],
        out_shape=[jax.ShapeDtypeStruct((bsz * seqlen, RWKV_WIDTH), BF16),
                   jax.ShapeDtypeStruct((bsz, RWKV_HEADS, HEAD_DIM, HEAD_DIM), F32)],
        scratch_shapes=[pltpu.VMEM((RWKV_HEADS, HEAD_DIM, HEAD_DIM), F32), pltpu.VMEM((c, RWKV_WIDTH), F32)],
        compiler_params=_cp("parallel", "arbitrary"),
        name="rwkv_scan_prompt",
    )(*seqs, rk, ln_g, ln_b)


def _rw_step_kernel(r_ref, lw_ref, k_ref, v_ref, kap_ref, beta_ref, g_ref, s0_ref, rk_ref, lng_ref, lnb_ref,
                    y_ref, sout_ref):
    bt = r_ref.shape[0]
    rows = _iota((bt, 1), 0)
    r, k, v = r_ref[...], k_ref[...], v_ref[...]
    dec = jnp.exp(lw_ref[...])
    kap, beta = kap_ref[...], beta_ref[...]
    seqs = range(bt)
    r_b = [jnp.where(rows == bi, r, 0.0) for bi in seqs]
    v_b = [jnp.where(rows == bi, v, 0.0) for bi in seqs]
    y_cols = []
    for h in range(RWKV_HEADS):
        hs = slice(h * HEAD_DIM, (h + 1) * HEAD_DIM)
        s0 = [s0_ref[bi, h] for bi in seqs]
        sa = [jnp.sum(s0[bi] * (-kap[bi:bi + 1, hs]), -1, keepdims=True) for bi in seqs]
        vk = [_dot3(v_b[bi][:, hs], k[:, hs], _TN) for bi in seqs]
        sn = [s0[bi] * dec[bi:bi + 1, hs] + sa[bi] * beta[bi:bi + 1, hs] + vk[bi] for bi in seqs]
        for bi in seqs:
            sout_ref[bi, h] = sn[bi]
        ys = [_dot3(r_b[bi][:, hs], sn[bi], _NT) for bi in seqs]
        y_cols.append(functools.reduce(lambda a, b: a + b, ys))
    y_all = jnp.concatenate(y_cols, axis=-1)
    y_ref[...] = _rw_post(y_all, r, k, v, g_ref[...], rk_ref[...], lng_ref[...], lnb_ref[...]).astype(y_ref.dtype)


def _rw_step_sample(seqs, s0, dbsz, rk, ln_g, ln_b, bt=8):
    tok = pl.BlockSpec((bt, RWKV_WIDTH), lambda i: (i, 0))
    par = pl.BlockSpec((1, RWKV_WIDTH), lambda i: (0, 0))
    st = pl.BlockSpec((bt, RWKV_HEADS, HEAD_DIM, HEAD_DIM), lambda i: (i, 0, 0, 0))
    return pl.pallas_call(
        _rw_step_kernel,
        grid=(dbsz // bt,),
        in_specs=[tok] * 7 + [st] + [par] * 3,
        out_specs=[tok, st],
        out_shape=[jax.ShapeDtypeStruct((dbsz, RWKV_WIDTH), F32),
                   jax.ShapeDtypeStruct((dbsz, RWKV_HEADS, HEAD_DIM, HEAD_DIM), F32)],
        compiler_params=_cp("parallel"),
        name="rwkv_step_sample",
    )(*seqs, s0, rk, ln_g, ln_b)


def _mla_proj_kernel(u_ref, cos_ref, sin_ref, qn_ref, kvn_ref, wqn_ref, wqr_ref, wqx_ref, wkn_ref,
                     ckv_ref, kpe_ref, q_ref, kcat_ref):
    tm = u_ref.shape[0]
    c_q = _rms_norm(u_ref[:, :MLA_Q_RANK], qn_ref[...])
    c_kv = _rms_norm(u_ref[:, MLA_Q_RANK:MLA_Q_RANK + MLA_KV_RANK], kvn_ref[...])
    cos = cos_ref[...]
    sin = sin_ref[...]
    o = MLA_Q_RANK + MLA_KV_RANK
    k_pe = u_ref[:, o:o + MLA_ROPE] * cos[:, :MLA_ROPE] + u_ref[:, o + MLA_ROPE:o + 2 * MLA_ROPE] * sin[:, :MLA_ROPE]
    ckv_ref[...] = c_kv
    kpe_ref[...] = k_pe
    zpad = jnp.zeros((tm, MLA_QK - MLA_KV_RANK - MLA_ROPE), F32)
    kcat_ref[...] = jnp.concatenate([c_kv, k_pe, zpad], axis=-1).astype(BF16)
    q_nope = _wdot(c_q, wqn_ref[...])
    q_pe = _wdot(c_q, wqr_ref[...]) * cos + _wdot(c_q, wqx_ref[...]) * sin
    for h in range(MLA_HEADS):
        q_lat = _wdot(q_nope[:, h * MLA_NOPE:(h + 1) * MLA_NOPE], wkn_ref[h])
        qh = jnp.concatenate([q_lat, q_pe[:, h * MLA_ROPE:(h + 1) * MLA_ROPE], zpad], axis=-1) * MLA_SCALE
        q_ref[h] = qh.astype(q_ref.dtype)


def _mla_proj(u_mla, cos5, sin5, qn, kvn, wqn, wqr, wqx, wkn, tm):
    t = u_mla.shape[0]
    tm = _row_tile(t, tm)
    full = lambda shape: pl.BlockSpec(shape, lambda i: (0,) * len(shape))
    hr = MLA_HEADS * MLA_ROPE
    return pl.pallas_call(
        _mla_proj_kernel,
        grid=(t // tm,),
        in_specs=[pl.BlockSpec((tm, MLA_UP), lambda i: (i, 0)),
                  pl.BlockSpec((tm, hr), lambda i: (i, 0)), pl.BlockSpec((tm, hr), lambda i: (i, 0)),
                  full((1, MLA_Q_RANK)), full((1, MLA_KV_RANK)),
                  full((MLA_Q_RANK, MLA_HEADS * MLA_NOPE)), full((MLA_Q_RANK, hr)), full((MLA_Q_RANK, hr)),
                  full((MLA_HEADS, MLA_NOPE, MLA_KV_RANK))],
        out_specs=[pl.BlockSpec((tm, MLA_KV_RANK), lambda i: (i, 0)),
                   pl.BlockSpec((tm, MLA_ROPE), lambda i: (i, 0)),
                   pl.BlockSpec((MLA_HEADS, tm, MLA_QK), lambda i: (0, i, 0)),
                   pl.BlockSpec((tm, MLA_QK), lambda i: (i, 0))],
        out_shape=[jax.ShapeDtypeStruct((t, MLA_KV_RANK), F32), jax.ShapeDtypeStruct((t, MLA_ROPE), F32),
                   jax.ShapeDtypeStruct((MLA_HEADS, t, MLA_QK), wqn.dtype), jax.ShapeDtypeStruct((t, MLA_QK), BF16)],
        compiler_params=_cp("parallel"),
        name="mla_proj",
    )(u_mla, cos5, sin5, qn, kvn, wqn, wqr, wqx, wkn)


def _mla_prompt_kernel(tq, tk, q_ref, k_ref, wv_ref, y_ref, m_scr, l_scr, acc_scr):
    i = pl.program_id(1)
    j = pl.program_id(2)
    rows = MLA_HEADS * tq
    last = (i * tq + tq - 1) // tk

    @pl.when(j == 0)
    def _():
        m_scr[...] = jnp.full(m_scr.shape, NEG_BIG, F32)
        l_scr[...] = jnp.zeros_like(l_scr)
        acc_scr[...] = jnp.zeros_like(acc_scr)

    @pl.when(j <= last)
    def _():
        q = q_ref[...].reshape(rows, MLA_QK)
        kc = k_ref[...]
        s = lax.dot_general(q, kc, (((1,), (1,)), ((), ())), preferred_element_type=F32)
        qpos = i * tq + _iota((rows, tk), 0) % tq
        kpos = j * tk + _iota((rows, tk), 1)
        s = jnp.where(kpos <= qpos, s, NEG_BIG)
        m_old = m_scr[...]
        m_new = jnp.maximum(m_old, jnp.max(s, -1, keepdims=True))
        alpha = jnp.exp(m_old - m_new)
        p = jnp.exp(s - m_new)
        l_scr[...] = alpha * l_scr[...] + jnp.sum(p, -1, keepdims=True)
        acc_scr[...] = alpha * acc_scr[...] + jnp.dot(p.astype(BF16), kc[:, :MLA_KV_RANK],
                                                      preferred_element_type=F32)
        m_scr[...] = m_new

    @pl.when(j == pl.num_programs(2) - 1)
    def _():
        o = (acc_scr[...] / l_scr[...]).reshape(MLA_HEADS, tq, MLA_KV_RANK)
        ys = [_dot(o[h], wv_ref[h]) for h in range(MLA_HEADS)]
        y_ref[...] = jnp.concatenate(ys, axis=-1).astype(y_ref.dtype)


def _mla_prompt_attend(q, kcat, wv, bsz, seqlen, tq=256, tk=512):
    nq = seqlen // tq
    nk = seqlen // tk
    rows = MLA_HEADS * tq

    def k_map(b, i, j):
        return (b * nk + jnp.minimum(j, (i * tq + tq - 1) // tk), 0)

    return pl.pallas_call(
        functools.partial(_mla_prompt_kernel, tq, tk),
        grid=(bsz, nq, nk),
        in_specs=[pl.BlockSpec((MLA_HEADS, tq, MLA_QK), lambda b, i, j: (0, b * nq + i, 0)),
                  pl.BlockSpec((tk, MLA_QK), k_map),
                  pl.BlockSpec((MLA_HEADS, MLA_KV_RANK, MLA_V), lambda b, i, j: (0, 0, 0))],
        out_specs=pl.BlockSpec((tq, MLA_WIDTH), lambda b, i, j: (b * nq + i, 0)),
        out_shape=jax.ShapeDtypeStruct((bsz * seqlen, MLA_WIDTH), BF16),
        scratch_shapes=[pltpu.VMEM((rows, 1), F32), pltpu.VMEM((rows, 1), F32),
                        pltpu.VMEM((rows, MLA_KV_RANK), F32)],
        compiler_params=_cp("parallel", "parallel", "arbitrary"),
        name="mla_prompt_attend",
    )(q, kcat, wv)


def _mla_sample_kernel(npg, pt_ref, q_ref, cnew_ref, pnew_ref, *refs):
    del pt_ref
    ckv_refs = refs[:npg]
    kpe_refs = refs[npg:2 * npg]
    o_ref = refs[2 * npg]
    m_scr, l_scr, acc_scr = refs[2 * npg + 1:]
    s_idx = pl.program_id(1)

    @pl.when(s_idx == 0)
    def _():
        m_scr[...] = jnp.full(m_scr.shape, NEG_BIG, F32)
        l_scr[...] = jnp.zeros_like(l_scr)
        acc_scr[...] = jnp.zeros_like(acc_scr)

    hp = q_ref.shape[0]

    def split(x):
        hi = x.astype(BF16)
        return jnp.concatenate([hi, (x - hi.astype(F32)).astype(BF16)], axis=0)

    q = q_ref[...]
    q2 = split(q)
    q_lat = q2[:, :MLA_KV_RANK]
    q_pe = q2[:, MLA_KV_RANK:MLA_KV_RANK + MLA_ROPE]
    cs = [r[...].astype(BF16) for r in ckv_refs]
    ss = [lax.dot_general(q_lat, c, _NT, preferred_element_type=F32)
          + jnp.dot(q_pe, r[...].astype(BF16), preferred_element_type=F32)
          for c, r in zip(cs, kpe_refs)]
    s2 = jnp.concatenate(ss, axis=-1)
    s = s2[:hp] + s2[hp:]
    m_old = m_scr[...]
    m_new = jnp.maximum(m_old, jnp.max(s, -1, keepdims=True))
    alpha = jnp.exp(m_old - m_new)
    p = jnp.exp(s - m_new)
    l_scr[...] = alpha * l_scr[...] + jnp.sum(p, -1, keepdims=True)
    p2 = split(p)
    pv = [jnp.dot(p2[:, n * PAGE_SIZE:(n + 1) * PAGE_SIZE], c, preferred_element_type=F32)
          for n, c in enumerate(cs)]
    pv = functools.reduce(lambda a, b: a + b, pv)
    acc_scr[...] = alpha * acc_scr[...] + pv[:hp] + pv[hp:]
    m_scr[...] = m_new

    @pl.when(s_idx == pl.num_programs(1) - 1)
    def _():
        cn = cnew_ref[...]
        pn = pnew_ref[...]
        s_new = (jnp.sum(q[:, :MLA_KV_RANK] * cn, -1, keepdims=True)
                 + jnp.sum(q[:, MLA_KV_RANK:MLA_KV_RANK + MLA_ROPE] * pn, -1, keepdims=True))
        m_fin = jnp.maximum(m_scr[...], s_new)
        a_fin = jnp.exp(m_scr[...] - m_fin)
        p_new = jnp.exp(s_new - m_fin)
        l_fin = a_fin * l_scr[...] + p_new
        o_ref[...] = (a_fin * acc_scr[...] + p_new * cn) / l_fin


def _mla_sample_attend(layer, q_s, c_new, p_new, cache_ckv, cache_kpe_t, page_table, npg=32):
    dbsz, n_pages = page_table.shape
    hp = q_s.shape[1]

    def page_map(n):
        return lambda b, s, pt: (layer, pt[b, s * npg + n], 0, 0)

    grid_spec = pltpu.PrefetchScalarGridSpec(
        num_scalar_prefetch=1,
        grid=(dbsz, n_pages // npg),
        in_specs=[pl.BlockSpec((None, hp, MLA_QK), lambda b, s, pt: (b, 0, 0)),
                  pl.BlockSpec((None, 1, MLA_KV_RANK), lambda b, s, pt: (b, 0, 0)),
                  pl.BlockSpec((None, 1, MLA_ROPE), lambda b, s, pt: (b, 0, 0))]
                 + [pl.BlockSpec((None, None, PAGE_SIZE, MLA_KV_RANK), page_map(n)) for n in range(npg)]
                 + [pl.BlockSpec((None, None, MLA_ROPE, PAGE_SIZE), page_map(n)) for n in range(npg)],
        out_specs=pl.BlockSpec((None, hp, MLA_KV_RANK), lambda b, s, pt: (b, 0, 0)),
        scratch_shapes=[pltpu.VMEM((hp, 1), F32), pltpu.VMEM((hp, 1), F32), pltpu.VMEM((hp, MLA_KV_RANK), F32)],
    )
    return pl.pallas_call(
        functools.partial(_mla_sample_kernel, npg),
        grid_spec=grid_spec,
        out_shape=jax.ShapeDtypeStruct((dbsz, hp, MLA_KV_RANK), F32),
        compiler_params=_cp("parallel", "arbitrary"),
        name="mla_sample_attend",
    )(page_table, q_s, c_new, p_new, *([cache_ckv] * npg), *([cache_kpe_t] * npg))


def _mla_vup_kernel(o_ref, wv_ref, y_ref):
    ys = [_wdot(o_ref[h], wv_ref[h]) for h in range(MLA_HEADS)]
    y_ref[...] = jnp.concatenate(ys, axis=-1).astype(y_ref.dtype)


def _mla_vup(o_heads, wv):
    hp, dbsz, _ = o_heads.shape
    return pl.pallas_call(
        _mla_vup_kernel,
        grid=(1,),
        in_specs=[pl.BlockSpec((hp, dbsz, MLA_KV_RANK), lambda i: (0, 0, 0)),
                  pl.BlockSpec((MLA_HEADS, MLA_KV_RANK, MLA_V), lambda i: (0, 0, 0))],
        out_specs=pl.BlockSpec((dbsz, MLA_WIDTH), lambda i: (0, 0)),
        out_shape=jax.ShapeDtypeStruct((dbsz, MLA_WIDTH), F32),
        compiler_params=_cp("arbitrary"),
        name="mla_vup",
    )(o_heads, wv)


def _rope_tables(pos):
    half = MLA_ROPE // 2
    inv = ROPE_THETA ** (-jnp.arange(half, dtype=F32) / half)
    ang = pos.astype(F32)[:, None] * inv[None, :]
    cos = jnp.tile(jnp.cos(ang), (1, 2 * MLA_HEADS))
    sin = jnp.tile(jnp.sin(ang), (1, 2 * MLA_HEADS))
    return cos, sin


def _pad_lanes(v, width=LANES):
    return jnp.pad(v, (0, width - v.shape[0])).reshape(1, width)


def _rotate_half_cols(w):
    half = MLA_ROPE // 2
    return jnp.concatenate([-w[..., half:], w[..., :half]], axis=-1)


def kernel(x_prompt, x_sample, cache_ckv, cache_kpe, page_table, state_ssm, state_conv, state_wkv, state_shift,
           w_in, w_out, ln1_g, ln1_b, ln2_g, ln2_b,
           ssd_conv_w, ssd_conv_b, ssd_dt_bias, ssd_a_log, ssd_d, ssd_norm,
           mla_q_norm, mla_kv_norm, mla_q_up, mla_kv_up,
           rw_mu, rw_w0, rw_w_up, rw_a0, rw_a_up, rw_g_up, rw_k_k, rw_k_a, rw_r_k, rw_ln_g, rw_ln_b,
           ffn_gu, ffn_down, moe_router, moe_gu, moe_down):
    bsz, seqlen, d = x_prompt.shape
    dbsz = x_sample.shape[0]
    n_p = bsz * seqlen
    past_len = page_table.shape[1] * PAGE_SIZE
    xp = x_prompt.reshape(n_p, d)
    xs = x_sample.reshape(dbsz, d)
    tm = 512
    cos_p, sin_p = _rope_tables(jnp.tile(jnp.arange(seqlen), bsz))
    cos_s, sin_s = _rope_tables(jnp.full((dbsz,), past_len))
    cache_kpe_t = jnp.swapaxes(cache_kpe, 2, 3)
    row = lambda v: v.reshape(1, -1)
    lo = lambda w: w.astype(BF16)

    outs_p = [[] for _ in range(6)]
    outs_s = [[] for _ in range(6)]
    for l in range(DEPTH):
        wi = w_in[l]
        m0 = SSD_IN
        r0 = SSD_IN + MLA_IN
        w_ssd = jnp.pad(wi[:, :SSD_IN], ((0, 0), (0, SSD_UP - SSD_IN)))
        k_rope_w = wi[:, m0 + 1024:m0 + MLA_IN]
        w_mla = jnp.concatenate([wi[:, m0:m0 + MLA_IN], _rotate_half_cols(k_rope_w)], axis=1)
        w_rw = wi[:, r0:]
        q_up = mla_q_up[l]
        wqn = q_up[:, :, :MLA_NOPE].reshape(MLA_Q_RANK, -1)
        wqr = q_up[:, :, MLA_NOPE:].reshape(MLA_Q_RANK, -1)
        wqx = _rotate_half_cols(q_up[:, :, MLA_NOPE:]).reshape(MLA_Q_RANK, -1)
        kv_up = mla_kv_up[l]
        wkn = jnp.transpose(kv_up[:, :, :MLA_NOPE], (1, 2, 0))
        wv = jnp.transpose(kv_up[:, :, MLA_NOPE:], (1, 0, 2))
        mla_w = (wqn, wqr, wqx, wkn)
        mla_n = (row(mla_q_norm[l]), row(mla_kv_norm[l]))
        ssd_par = (ssd_conv_w[l], row(ssd_conv_b[l]), _pad_lanes(ssd_dt_bias[l]), _pad_lanes(ssd_a_log[l]))
        d_rep = row(jnp.repeat(ssd_d[l], HEAD_DIM))
        a_rep = row(jnp.repeat(-jnp.exp(ssd_a_log[l]), HEAD_DIM))
        rw_par_s = (row(rw_mu[l]), row(rw_w0[l]), rw_w_up[l], row(rw_a0[l]), rw_a_up[l], rw_g_up[l],
                    row(rw_k_k[l]), row(rw_k_a[l]))
        rw_par_p = tuple(lo(w) if w.shape[0] > 1 else w for w in rw_par_s)
        rw_post = (row(rw_r_k[l]), row(rw_ln_g[l]), row(rw_ln_b[l]))
        ln1 = (row(ln1_g[l]), row(ln1_b[l]))
        ln2 = (row(ln2_g[l]), row(ln2_b[l]))

        u_ssd = _matmul(xp, lo(w_ssd), tm)
        u_mla = _matmul(xp, lo(w_mla), tm)
        u_rw = _matmul(xp, lo(w_rw), tm)
        y_ssd_p, ssm_p = _ssd_prompt(u_ssd, bsz, seqlen, *ssd_par, d_rep, row(ssd_norm[l]))
        conv_p = u_ssd[:, SSD_WIDTH:SSD_WIDTH + SSD_XBC].reshape(bsz, seqlen, SSD_XBC)[:, seqlen - (SSD_CONV - 1):]
        ckv_p, kpe_p, q_p, kcat = _mla_proj(u_mla, cos_p, sin_p, *mla_n, *[lo(w) for w in mla_w], tm)
        y_mla_p = _mla_prompt_attend(q_p, kcat, lo(wv), bsz, seqlen)
        seq_p = _rw_prep_prompt(u_rw, n_p, seqlen, rw_par_p)
        y_rw_p, wkv_p = _rw_scan_prompt(seq_p, bsz, seqlen, *rw_post)
        shift_p = u_rw.reshape(bsz, seqlen, RWKV_IN)[:, -1]

        us_ssd = _matmul(xs, w_ssd, dbsz)
        us_mla = _matmul(xs, w_mla, dbsz)
        us_rw = _matmul(xs, w_rw, dbsz)
        y_ssd_s, ssm_s = _ssd_sample(l, us_ssd, 0, dbsz, state_conv[l].reshape(dbsz, -1), state_ssm,
                                     *ssd_par, a_rep, d_rep, row(ssd_norm[l]))
        conv_s = jnp.concatenate([state_conv[l][:, 1:], us_ssd[:, None, SSD_WIDTH:SSD_WIDTH + SSD_XBC]], axis=1)
        ckv_s, kpe_s, q_s, _ = _mla_proj(us_mla, cos_s, sin_s, *mla_n, *mla_w, dbsz)
        q_s = jnp.pad(jnp.transpose(q_s, (1, 0, 2)), ((0, 0), (0, 8 - MLA_HEADS), (0, 0)))
        o_lat = _mla_sample_attend(l, q_s, ckv_s[:, None, :], kpe_s[:, None, :], cache_ckv, cache_kpe_t, page_table)
        y_mla_s = _mla_vup(jnp.transpose(o_lat, (1, 0, 2)), wv)
        seq_s = _rw_prep_sample(us_rw, 0, dbsz, state_shift[l], rw_par_s)
        y_rw_s, wkv_s = _rw_step_sample(seq_s, state_wkv[l], dbsz, *rw_post)

        xp = _matmul_res_ln(jnp.concatenate([y_ssd_p, y_mla_p, y_rw_p], axis=-1), lo(w_out[l]), xp, *ln1, tm)
        xs = _matmul_res_ln(jnp.concatenate([y_ssd_s, y_mla_s, y_rw_s], axis=-1), w_out[l], xs, *ln1, dbsz)
        if l % 2 == 1:
            wr = jnp.pad(moe_router[l // 2], ((0, 0), (0, LANES - N_EXPERTS)))
            w_gu, w_dn = lo(moe_gu[l // 2]), lo(moe_down[l // 2])
            xp = _moe_prompt(xp, _router(xp, wr, tm), w_gu, w_dn, *ln2, tm, 256, MOE_CAP)
            xs = _moe_dense(xs, _router(xs, wr, dbsz), w_gu, w_dn, *ln2, dbsz, 256)
        else:
            xp = _ffn_dense(xp, lo(ffn_gu[l // 2]), lo(ffn_down[l // 2]), *ln2, tm, 512)
            xs = _ffn_dense(xs, ffn_gu[l // 2], ffn_down[l // 2], *ln2, dbsz, 512)

        for lst, val in zip(outs_p, (ckv_p.reshape(bsz, seqlen, -1), kpe_p.reshape(bsz, seqlen, -1),
                                     ssm_p, conv_p, wkv_p, shift_p)):
            lst.append(val)
        for lst, val in zip(outs_s, (ckv_s.reshape(dbsz, 1, -1), kpe_s.reshape(dbsz, 1, -1),
                                     ssm_s, conv_s, wkv_s, us_rw)):
            lst.append(val)

    ckv_p, kpe_p, ssm_p, conv_p, wkv_p, shift_p = [jnp.stack(o) for o in outs_p]
    ckv_s, kpe_s, ssm_s, conv_s, wkv_s, shift_s = [jnp.stack(o) for o in outs_s]
    return (xp.reshape(bsz, seqlen, d), xs.reshape(dbsz, 1, d), ckv_p, kpe_p, ckv_s, kpe_s, ssm_p, ssm_s,
            conv_p, conv_s, wkv_p, wkv_s, shift_p, shift_s)
```

```python
import functools
import math

import jax
import jax.numpy as jnp
from jax import lax
from jax.experimental import pallas as pl
from jax.experimental.pallas import tpu as pltpu

F32 = jnp.float32
BF16 = jnp.bfloat16
HI = lax.Precision.HIGHEST

D_MODEL = 2048
PAGE_SIZE = 128
HEAD_DIM = 64
SSD_WIDTH = 768
SSD_HEADS = 12
SSD_GROUPS = 2
SSD_STATE = 128
SSD_CONV = 4
SSD_CHUNK = 128
SSD_XBC = 1280
SSD_IN = 2060
SSD_GW = SSD_WIDTH // SSD_GROUPS
SSD_UP = 2176
MLA_HEADS = 5
MLA_NOPE = 128
MLA_ROPE = 64
MLA_V = 128
MLA_WIDTH = 640
MLA_Q_RANK = 512
MLA_KV_RANK = 512
MLA_IN = 1088
MLA_UP = 1152
MLA_QK = 640
MLA_SCALE = (MLA_NOPE + MLA_ROPE) ** -0.5
ROPE_THETA = 10000.0
RWKV_WIDTH = 640
RWKV_HEADS = 10
RWKV_IN = 2176
RWKV_GN_EPS = 64e-5
RWKV_CHUNK = 64
RWKV_SUB = 16
FFN_DENSE = 5632
N_EXPERTS = 8
FFN_EXPERT = 2816
MOE_CAP = 192
DEPTH = 2
ALPHA = (2.0 * DEPTH) ** 0.25
LN_EPS = 1e-5
RMS_EPS = 1e-6
LANES = 128
NEG_BIG = -1e30

VMEM_LIMIT = 56 * 1024 * 1024


def _cp(*sem):
    return pltpu.CompilerParams(dimension_semantics=sem, vmem_limit_bytes=VMEM_LIMIT)


def _dot(a, b):
    return jnp.dot(a.astype(BF16), b.astype(BF16), preferred_element_type=F32)


def _dot_nt(a, b):
    return lax.dot_general(a.astype(BF16), b.astype(BF16), (((1,), (1,)), ((), ())), preferred_element_type=F32)


def _dot_tn(a, b):
    return lax.dot_general(a.astype(BF16), b.astype(BF16), (((0,), (0,)), ((), ())), preferred_element_type=F32)


def _dot_hi(a, b):
    return jnp.dot(a, b, precision=HI, preferred_element_type=F32)


_NN = (((1,), (0,)), ((), ()))
_NT = (((1,), (1,)), ((), ()))
_TN = (((0,), (0,)), ((), ()))


def _dot3(a, b, dims):
    ah = a.astype(BF16)
    bh = b.astype(BF16)
    al = (a - ah.astype(F32)).astype(BF16)
    bl = (b - bh.astype(F32)).astype(BF16)
    out = lax.dot_general(ah, bh, dims, preferred_element_type=F32)
    out = out + lax.dot_general(ah, bl, dims, preferred_element_type=F32)
    return out + lax.dot_general(al, bh, dims, preferred_element_type=F32)


def _sigmoid(x):
    return 1.0 / (1.0 + jnp.exp(-x))


def _silu(x):
    return x * _sigmoid(x)


def _softplus(x):
    return jnp.maximum(x, 0.0) + jnp.log1p(jnp.exp(-jnp.abs(x)))


def _iota(shape, dim):
    return lax.broadcasted_iota(jnp.int32, shape, dim)


def _seg_sum(x, width, seg):
    same = (_iota((width, width), 0) // seg) == (_iota((width, width), 1) // seg)
    ones = jnp.where(same, 1.0, 0.0).astype(BF16)
    hi = x.astype(BF16)
    lo = (x - hi.astype(F32)).astype(BF16)
    return (jnp.dot(hi, ones, preferred_element_type=F32) + jnp.dot(lo, ones, preferred_element_type=F32))


def _layer_norm(x, g, b):
    mu = jnp.mean(x, -1, keepdims=True)
    xc = x - mu
    var = jnp.mean(xc * xc, -1, keepdims=True)
    return xc * lax.rsqrt(var + LN_EPS) * g + b


def _rms_norm(x, g):
    return x * lax.rsqrt(jnp.mean(x * x, -1, keepdims=True) + RMS_EPS) * g


def _wdot(x, w):
    if w.dtype == BF16:
        return jnp.dot(x.astype(BF16), w, preferred_element_type=F32)
    return jnp.dot(x.astype(F32), w, precision=HI, preferred_element_type=F32)


def _row_tile(rows, pref):
    return pref if rows % pref == 0 else rows


def _mm_kernel(x_ref, w_ref, o_ref):
    o_ref[...] = _wdot(x_ref[...], w_ref[...]).astype(o_ref.dtype)


def _matmul(x, w, tm, out_dtype=F32):
    m, k = x.shape
    n = w.shape[1]
    tm = _row_tile(m, tm)
    return pl.pallas_call(
        _mm_kernel,
        grid=(m // tm,),
        in_specs=[pl.BlockSpec((tm, k), lambda i: (i, 0)), pl.BlockSpec((k, n), lambda i: (0, 0))],
        out_specs=pl.BlockSpec((tm, n), lambda i: (i, 0)),
        out_shape=jax.ShapeDtypeStruct((m, n), out_dtype),
        compiler_params=_cp("parallel"),
        name="matmul",
    )(x, w)


def _mm_res_ln_kernel(a_ref, w_ref, res_ref, g_ref, b_ref, o_ref):
    mix = _wdot(a_ref[...], w_ref[...])
    o_ref[...] = _layer_norm(ALPHA * res_ref[...] + mix, g_ref[...], b_ref[...])


def _matmul_res_ln(a, w, res, g, b, tm):
    m, k = a.shape
    n = w.shape[1]
    tm = _row_tile(m, tm)
    return pl.pallas_call(
        _mm_res_ln_kernel,
        grid=(m // tm,),
        in_specs=[pl.BlockSpec((tm, k), lambda i: (i, 0)), pl.BlockSpec((k, n), lambda i: (0, 0)),
                  pl.BlockSpec((tm, n), lambda i: (i, 0)), pl.BlockSpec((1, n), lambda i: (0, 0)),
                  pl.BlockSpec((1, n), lambda i: (0, 0))],
        out_specs=pl.BlockSpec((tm, n), lambda i: (i, 0)),
        out_shape=jax.ShapeDtypeStruct((m, n), F32),
        compiler_params=_cp("parallel"),
        name="out_proj_ln",
    )(a, w, res, g, b)


def _ffn_kernel(x_ref, wg_ref, wu_ref, wd_ref, g_ref, b_ref, o_ref, xb_scr):
    f = pl.program_id(1)

    @pl.when(f == 0)
    def _():
        xb_scr[...] = x_ref[...].astype(xb_scr.dtype)
        o_ref[...] = ALPHA * x_ref[...]

    xb = xb_scr[...]
    act = _silu(_wdot(xb, wg_ref[...])) * _wdot(xb, wu_ref[...])
    o_ref[...] += _wdot(act, wd_ref[...])

    @pl.when(f == pl.num_programs(1) - 1)
    def _():
        o_ref[...] = _layer_norm(o_ref[...], g_ref[...], b_ref[...])


def _ffn_dense(x, w_gu, w_down, g, b, tm, tf):
    m, d = x.shape
    fdim = w_down.shape[0]
    nf = fdim // tf
    tm = _row_tile(m, tm)
    return pl.pallas_call(
        _ffn_kernel,
        grid=(m // tm, nf),
        in_specs=[pl.BlockSpec((tm, d), lambda i, f: (i, 0)),
                  pl.BlockSpec((d, tf), lambda i, f: (0, f)),
                  pl.BlockSpec((d, tf), lambda i, f: (0, nf + f)),
                  pl.BlockSpec((tf, d), lambda i, f: (f, 0)),
                  pl.BlockSpec((1, d), lambda i, f: (0, 0)),
                  pl.BlockSpec((1, d), lambda i, f: (0, 0))],
        out_specs=pl.BlockSpec((tm, d), lambda i, f: (i, 0)),
        out_shape=jax.ShapeDtypeStruct((m, d), F32),
        scratch_shapes=[pltpu.VMEM((tm, d), w_gu.dtype)],
        compiler_params=_cp("parallel", "arbitrary"),
        name="ffn_dense",
    )(x, w_gu, w_gu, w_down, g, b)


def _router_kernel(x_ref, wr_ref, gates_ref):
    logits = _dot_hi(x_ref[...], wr_ref[...])
    lane = _iota(logits.shape, 1)
    valid = lane < N_EXPERTS
    logits = jnp.where(valid, logits, NEG_BIG)
    mx = jnp.max(logits, -1, keepdims=True)
    ex = jnp.where(valid, jnp.exp(logits - mx), 0.0)
    probs = ex / jnp.sum(ex, -1, keepdims=True)
    p1 = jnp.max(probs, -1, keepdims=True)
    i1 = jnp.min(jnp.where(probs == p1, lane, LANES), -1, keepdims=True)
    first = lane == i1
    rest = jnp.where(first | (~valid), -1.0, probs)
    p2 = jnp.max(rest, -1, keepdims=True)
    i2 = jnp.min(jnp.where(rest == p2, lane, LANES), -1, keepdims=True)
    second = lane == i2
    tot = p1 + p2
    gates_ref[...] = jnp.where(first, p1 / tot, 0.0) + jnp.where(second, p2 / tot, 0.0)


def _router(x, wr_pad, tm):
    m, d = x.shape
    tm = _row_tile(m, tm)
    return pl.pallas_call(
        _router_kernel,
        grid=(m // tm,),
        in_specs=[pl.BlockSpec((tm, d), lambda i: (i, 0)), pl.BlockSpec((d, LANES), lambda i: (0, 0))],
        out_specs=pl.BlockSpec((tm, LANES), lambda i: (i, 0)),
        out_shape=jax.ShapeDtypeStruct((m, LANES), F32),
        compiler_params=_cp("parallel"),
        name="moe_router",
    )(x, wr_pad)


def _moe_kernel(x_ref, gates_ref, wg_ref, wu_ref, wd_ref, g_ref, b_ref, o_ref, xb_scr):
    e = pl.program_id(1)
    f = pl.program_id(2)

    @pl.when((e == 0) & (f == 0))
    def _():
        xb_scr[...] = x_ref[...].astype(BF16)
        o_ref[...] = ALPHA * x_ref[...]

    gates = gates_ref[...]
    gate = jnp.sum(jnp.where(_iota(gates.shape, 1) == e, gates, 0.0), -1, keepdims=True)
    xb = xb_scr[...]
    hg = jnp.dot(xb, wg_ref[...], preferred_element_type=F32)
    hu = jnp.dot(xb, wu_ref[...], preferred_element_type=F32)
    act = (_silu(hg) * hu * gate).astype(BF16)
    o_ref[...] += jnp.dot(act, wd_ref[...], preferred_element_type=F32)

    @pl.when((e == pl.num_programs(1) - 1) & (f == pl.num_programs(2) - 1))
    def _():
        o_ref[...] = _layer_norm(o_ref[...], g_ref[...], b_ref[...])


def _moe_dense(x, gates, w_gu, w_down, g, b, tm, tf):
    m, d = x.shape
    ne, fdim, _ = w_down.shape
    tm = _row_tile(m, tm)
    nf = fdim // tf
    return pl.pallas_call(
        _moe_kernel,
        grid=(m // tm, ne, nf),
        in_specs=[pl.BlockSpec((tm, d), lambda i, e, f: (i, 0)),
                  pl.BlockSpec((tm, LANES), lambda i, e, f: (i, 0)),
                  pl.BlockSpec((None, None, None, d, tf), lambda i, e, f: (e, 0, f, 0, 0)),
                  pl.BlockSpec((None, None, None, d, tf), lambda i, e, f: (e, 1, f, 0, 0)),
                  pl.BlockSpec((None, tf, d), lambda i, e, f: (e, f, 0)),
                  pl.BlockSpec((1, d), lambda i, e, f: (0, 0)),
                  pl.BlockSpec((1, d), lambda i, e, f: (0, 0))],
        out_specs=pl.BlockSpec((tm, d), lambda i, e, f: (i, 0)),
        out_shape=jax.ShapeDtypeStruct((m, d), F32),
        scratch_shapes=[pltpu.VMEM((tm, d), BF16)],
        compiler_params=_cp("parallel", "arbitrary", "arbitrary"),
        name="moe_dense",
    )(x, gates, w_gu, w_gu, w_down, g, b)


def _moe_compact_kernel(x_ref, rank_ref, gate_ref, wg_ref, wu_ref, wd_ref, g_ref, b_ref, o_ref,
                        xb_scr, sel_scr, xg_scr, gc_scr, acc_scr):
    e = pl.program_id(1)
    f = pl.program_id(2)
    cap, tm = sel_scr.shape

    @pl.when((e == 0) & (f == 0))
    def _():
        xb_scr[...] = x_ref[...].astype(BF16)
        o_ref[...] = ALPHA * x_ref[...]

    @pl.when(f == 0)
    def _():
        rank = rank_ref[pl.ds(e, 1), :]
        hit = _iota((cap, tm), 0).astype(F32) == rank
        sel = jnp.where(hit, 1.0, 0.0).astype(BF16)
        sel_scr[...] = sel
        xg_scr[...] = jnp.dot(sel, xb_scr[...], preferred_element_type=F32).astype(BF16)
        gate = jnp.sum(jnp.where(hit, gate_ref[pl.ds(e, 1), :], 0.0), -1, keepdims=True)
        gc_scr[...] = jnp.broadcast_to(gate, gc_scr.shape)
        acc_scr[...] = jnp.zeros_like(acc_scr)

    xg = xg_scr[...]
    hg = jnp.dot(xg, wg_ref[...], preferred_element_type=F32)
    hu = jnp.dot(xg, wu_ref[...], preferred_element_type=F32)
    act = (_silu(hg) * hu * gc_scr[:, 0:1]).astype(BF16)
    acc_scr[...] += jnp.dot(act, wd_ref[...], preferred_element_type=F32)

    @pl.when(f == pl.num_programs(2) - 1)
    def _():
        acc = acc_scr[...]
        hi = acc.astype(BF16)
        lo = (acc - hi.astype(F32)).astype(BF16)
        sel = sel_scr[...]
        o_ref[...] += (lax.dot_general(sel, hi, _TN, preferred_element_type=F32)
                       + lax.dot_general(sel, lo, _TN, preferred_element_type=F32))

    @pl.when((e == pl.num_programs(1) - 1) & (f == pl.num_programs(2) - 1))
    def _():
        o_ref[...] = _layer_norm(o_ref[...], g_ref[...], b_ref[...])


def _moe_compact(x, rank_t, gate_t, w_gu, w_down, g, b, tm, tf, cap):
    m, d = x.shape
    ne, fdim, _ = w_down.shape
    nf = fdim // tf
    return pl.pallas_call(
        _moe_compact_kernel,
        grid=(m // tm, ne, nf),
        in_specs=[pl.BlockSpec((tm, d), lambda i, e, f: (i, 0)),
                  pl.BlockSpec((None, ne, tm), lambda i, e, f: (i, 0, 0)),
                  pl.BlockSpec((None, ne, tm), lambda i, e, f: (i, 0, 0)),
                  pl.BlockSpec((None, None, None, d, tf), lambda i, e, f: (e, 0, f, 0, 0)),
                  pl.BlockSpec((None, None, None, d, tf), lambda i, e, f: (e, 1, f, 0, 0)),
                  pl.BlockSpec((None, tf, d), lambda i, e, f: (e, f, 0)),
                  pl.BlockSpec((1, d), lambda i, e, f: (0, 0)),
                  pl.BlockSpec((1, d), lambda i, e, f: (0, 0))],
        out_specs=pl.BlockSpec((tm, d), lambda i, e, f: (i, 0)),
        out_shape=jax.ShapeDtypeStruct((m, d), F32),
        scratch_shapes=[pltpu.VMEM((tm, d), BF16), pltpu.VMEM((cap, tm), BF16), pltpu.VMEM((cap, d), BF16),
                        pltpu.VMEM((cap, LANES), F32), pltpu.VMEM((cap, d), F32)],
        compiler_params=_cp("parallel", "arbitrary", "arbitrary"),
        name="moe_compact",
    )(x, rank_t, gate_t, w_gu, w_gu, w_down, g, b)


def _moe_prompt(x, gates, w_gu, w_down, g, b, tm, tf, cap):
    m = x.shape[0]
    if m % tm != 0 or cap >= tm:
        return _moe_dense(x, gates, w_gu, w_down, g, b, tm, tf)
    nt = m // tm
    gate_e = gates[:, :N_EXPERTS].reshape(nt, tm, N_EXPERTS)
    routed = gate_e > 0.0
    slot = jnp.cumsum(routed.astype(jnp.int32), axis=1) - 1
    rank_t = jnp.swapaxes(jnp.where(routed, slot, -1).astype(F32), 1, 2)
    gate_t = jnp.swapaxes(gate_e, 1, 2)
    fits = jnp.max(jnp.sum(routed.astype(jnp.int32), axis=1)) <= cap
    return lax.cond(fits,
                    lambda: _moe_compact(x, rank_t, gate_t, w_gu, w_down, g, b, tm, tf, cap),
                    lambda: _moe_dense(x, gates, w_gu, w_down, g, b, tm, tf))


def _ssd_gate_norm(y, xs, z, d_rep, norm_w):
    y = (y + d_rep * xs) * _silu(z)
    outs = []
    for g in range(SSD_GROUPS):
        sl = slice(g * SSD_GW, (g + 1) * SSD_GW)
        outs.append(_rms_norm(y[:, sl], norm_w[:, sl]))
    return jnp.concatenate(outs, axis=-1)


def _ssd_prompt_kernel(u_ref, cw_ref, cb_ref, dtb_ref, alog_ref, drep_ref, nw_ref,
                       y_ref, hout_ref, xf_scr, h_scr):
    q = SSD_CHUNK
    c = pl.program_id(1)

    @pl.when(c == 0)
    def _():
        xf_scr[0:8, :] = jnp.zeros((8, SSD_XBC), F32)
        h_scr[...] = jnp.zeros_like(h_scr)

    xf_scr[8:8 + q, :] = u_ref[:, SSD_WIDTH:SSD_WIDTH + SSD_XBC]
    acc = cb_ref[...] + xf_scr[5:5 + q, :] * cw_ref[0:1, :]
    for k in range(1, SSD_CONV):
        acc = acc + xf_scr[5 + k:5 + k + q, :] * cw_ref[k:k + 1, :]
    xf_scr[5:8, :] = xf_scr[5 + q:8 + q, :]
    xbc = _silu(acc)
    xs = xbc[:, :SSD_WIDTH]
    bmat = xbc[:, SSD_WIDTH:SSD_WIDTH + SSD_GROUPS * SSD_STATE]
    cmat = xbc[:, SSD_WIDTH + SSD_GROUPS * SSD_STATE:]

    dt = _softplus(u_ref[:, SSD_WIDTH + SSD_XBC:] + dtb_ref[...])
    da = dt * (-jnp.exp(alog_ref[...]))
    row = _iota((q, q), 0)
    col = _iota((q, q), 1)
    causal = row >= col
    acum = _dot_hi(jnp.where(causal, 1.0, 0.0), da)
    dt_t = dt.T
    acum_t = _dot_hi(da.T, jnp.where(row <= col, 1.0, 0.0))

    scores = [_dot_nt(cmat[:, g * SSD_STATE:(g + 1) * SSD_STATE], bmat[:, g * SSD_STATE:(g + 1) * SSD_STATE])
              for g in range(SSD_GROUPS)]
    rep = SSD_HEADS // SSD_GROUPS
    heads = range(SSD_HEADS)
    gsl = [slice((h // rep) * SSD_STATE, (h // rep + 1) * SSD_STATE) for h in heads]
    a_col = [acum[:, h:h + 1] for h in heads]
    a_last = [acum[q - 1:q, h:h + 1] for h in heads]
    xs_h = [xs[:, h * HEAD_DIM:(h + 1) * HEAD_DIM] for h in heads]
    h_prev = [h_scr[h] for h in heads]
    m = [scores[h // rep] * jnp.exp(jnp.where(causal, a_col[h] - acum_t[h:h + 1, :], NEG_BIG)) * dt_t[h:h + 1, :]
         for h in heads]
    y_diag = [_dot(m[h], xs_h[h]) for h in heads]
    y_off = [jnp.exp(a_col[h]) * _dot_nt(cmat[:, gsl[h]], h_prev[h]) for h in heads]
    xw = [xs_h[h] * (jnp.exp(a_last[h] - a_col[h]) * dt[:, h:h + 1]) for h in heads]
    h_new = [jnp.exp(a_last[h]) * h_prev[h] + _dot_tn(xw[h], bmat[:, gsl[h]]) for h in heads]
    for h in heads:
        h_scr[h] = h_new[h]
    y_all = jnp.concatenate([y_diag[h] + y_off[h] for h in heads], axis=-1)

    y_ref[...] = _ssd_gate_norm(y_all, xs, u_ref[:, :SSD_WIDTH], drep_ref[...], nw_ref[...]).astype(y_ref.dtype)

    @pl.when(c == pl.num_programs(1) - 1)
    def _():
        hout_ref[...] = h_scr[...]


def _ssd_prompt(u_ssd, bsz, seqlen, cw, cb, dtb, alog, drep, nw):
    q = SSD_CHUNK
    nc = seqlen // q
    full = lambda shape: pl.BlockSpec(shape, lambda b, c: (0,) * len(shape))
    return pl.pallas_call(
        _ssd_prompt_kernel,
        grid=(bsz, nc),
        in_specs=[pl.BlockSpec((q, SSD_UP), lambda b, c: (b * nc + c, 0)),
                  full((SSD_CONV, SSD_XBC)), full((1, SSD_XBC)), full((1, LANES)), full((1, LANES)),
                  full((1, SSD_WIDTH)), full((1, SSD_WIDTH))],
        out_specs=[pl.BlockSpec((q, SSD_WIDTH), lambda b, c: (b * nc + c, 0)),
                   pl.BlockSpec((None, SSD_HEADS, HEAD_DIM, SSD_STATE), lambda b, c: (b, 0, 0, 0))],
        out_shape=[jax.ShapeDtypeStruct((bsz * seqlen, SSD_WIDTH), BF16),
                   jax.ShapeDtypeStruct((bsz, SSD_HEADS, HEAD_DIM, SSD_STATE), F32)],
        scratch_shapes=[pltpu.VMEM((8 + q, SSD_XBC), F32),
                        pltpu.VMEM((SSD_HEADS, HEAD_DIM, SSD_STATE), F32)],
        compiler_params=_cp("parallel", "arbitrary"),
        name="ssd_prompt",
    )(u_ssd, cw, cb, dtb, alog, drep, nw)


def _ssd_sample_kernel(u_ref, cbuf_ref, h0_ref, cw_ref, cb_ref, dtb_ref, alog_ref, arep_ref, drep_ref, nw_ref,
                       y_ref, hout_ref):
    bt = u_ref.shape[0]
    x_new = u_ref[:, SSD_WIDTH:SSD_WIDTH + SSD_XBC]
    acc = cb_ref[...] + x_new * cw_ref[SSD_CONV - 1:SSD_CONV, :]
    for k in range(SSD_CONV - 1):
        acc = acc + cbuf_ref[:, k * SSD_XBC:(k + 1) * SSD_XBC] * cw_ref[k:k + 1, :]
    xbc = _silu(acc)
    xs = xbc[:, :SSD_WIDTH]
    bmat = xbc[:, SSD_WIDTH:SSD_WIDTH + SSD_GROUPS * SSD_STATE]
    cmat = xbc[:, SSD_WIDTH + SSD_GROUPS * SSD_STATE:]
    dt = _softplus(u_ref[:, SSD_WIDTH + SSD_XBC:] + dtb_ref[...])
    expand = jnp.where(_iota((LANES, SSD_WIDTH), 0) == _iota((LANES, SSD_WIDTH), 1) // HEAD_DIM, 1.0, 0.0)
    dt_rep = _dot_hi(dt, expand)
    dec_rep = jnp.exp(dt_rep * arep_ref[...])
    dtx = dt_rep * xs
    rows = _iota((bt, 1), 0)
    ones = jnp.ones((bt, SSD_STATE), F32)
    rep = SSD_HEADS // SSD_GROUPS
    seqs = range(bt)
    sel = [rows == bi for bi in seqs]
    y_cols = []
    for g in range(SSD_GROUPS):
        cs = slice(g * SSD_GW, (g + 1) * SSD_GW)
        gs = slice(g * SSD_STATE, (g + 1) * SSD_STATE)
        hsl = slice(g * rep, (g + 1) * rep)
        h0 = [h0_ref[bi, hsl].reshape(SSD_GW, SSD_STATE) for bi in seqs]
        outer = [_dot3(jnp.where(sel[bi], dtx[:, cs], 0.0), bmat[:, gs], _TN) for bi in seqs]
        decm = [_dot3(jnp.where(sel[bi], dec_rep[:, cs], 0.0), ones, _TN) for bi in seqs]
        hn = [decm[bi] * h0[bi] + outer[bi] for bi in seqs]
        for bi in seqs:
            hout_ref[bi, hsl] = hn[bi].reshape(rep, HEAD_DIM, SSD_STATE)
        ys = [_dot3(jnp.where(sel[bi], cmat[:, gs], 0.0), hn[bi], _NT) for bi in seqs]
        y_cols.append(functools.reduce(lambda a, b: a + b, ys))
    y_all = jnp.concatenate(y_cols, axis=-1)
    y_ref[...] = _ssd_gate_norm(y_all, xs, u_ref[:, :SSD_WIDTH], drep_ref[...], nw_ref[...]).astype(y_ref.dtype)


def _ssd_sample(layer, u_ssd, row0, dbsz, conv_flat, h0, cw, cb, dtb, alog, arep, drep, nw, bt=8):
    full = lambda shape: pl.BlockSpec(shape, lambda i: (0,) * len(shape))
    blk0 = row0 // bt
    return pl.pallas_call(
        _ssd_sample_kernel,
        grid=(dbsz // bt,),
        in_specs=[pl.BlockSpec((bt, SSD_UP), lambda i: (blk0 + i, 0)),
                  pl.BlockSpec((bt, (SSD_CONV - 1) * SSD_XBC), lambda i: (i, 0)),
                  pl.BlockSpec((None, bt, SSD_HEADS, HEAD_DIM, SSD_STATE), lambda i: (layer, i, 0, 0, 0)),
                  full((SSD_CONV, SSD_XBC)), full((1, SSD_XBC)), full((1, LANES)), full((1, LANES)),
                  full((1, SSD_WIDTH)), full((1, SSD_WIDTH)), full((1, SSD_WIDTH))],
        out_specs=[pl.BlockSpec((bt, SSD_WIDTH), lambda i: (i, 0)),
                   pl.BlockSpec((bt, SSD_HEADS, HEAD_DIM, SSD_STATE), lambda i: (i, 0, 0, 0))],
        out_shape=[jax.ShapeDtypeStruct((dbsz, SSD_WIDTH), F32),
                   jax.ShapeDtypeStruct((dbsz, SSD_HEADS, HEAD_DIM, SSD_STATE), F32)],
        compiler_params=_cp("parallel"),
        name="ssd_sample",
    )(u_ssd, conv_flat, h0, cw, cb, dtb, alog, arep, drep, nw)


def _rw_prep_body(u, prev, mu, w0, w_up, a0, a_up, g_up, k_k, k_a):
    o1, o2, o3 = RWKV_WIDTH, 2 * RWKV_WIDTH, 3 * RWKV_WIDTH
    o4, o5 = o3 + 64, o3 + 128
    ux = u + (prev - u) * mu
    r, k, v = ux[:, :o1], ux[:, o1:o2], ux[:, o2:o3]
    xw, xa, xg = ux[:, o3:o4], ux[:, o4:o5], ux[:, o5:]
    w = -_softplus(-(w0 + _wdot(jnp.tanh(xw), w_up))) - 0.5
    lw = -jnp.exp(w)
    a = _sigmoid(a0 + _wdot(xa, a_up))
    g = _wdot(_sigmoid(xg), g_up)
    kk = k * k_k
    nrm = jnp.maximum(jnp.sqrt(_seg_sum(kk * kk, RWKV_WIDTH, HEAD_DIM)), 1e-12)
    kap = kk / nrm
    k = k * (1.0 + (a - 1.0) * k_a)
    return r, lw, k, v, kap, kap * a, g


def _rw_prep_prompt_kernel(tiles_per_seq, u_ref, mu_ref, w0_ref, wup_ref, a0_ref, aup_ref, gup_ref, kk_ref, ka_ref,
                           r_ref, lw_ref, k_ref, v_ref, kap_ref, beta_ref, g_ref, scr):
    tm = u_ref.shape[0]
    i = pl.program_id(0)

    @pl.when(i % tiles_per_seq == 0)
    def _():
        scr[0:8, :] = jnp.zeros((8, RWKV_IN), F32)

    scr[8:8 + tm, :] = u_ref[...]
    prev = scr[7:7 + tm, :]
    outs = _rw_prep_body(u_ref[...], prev, mu_ref[...], w0_ref[...], wup_ref[...], a0_ref[...], aup_ref[...],
                         gup_ref[...], kk_ref[...], ka_ref[...])
    scr[7:8, :] = scr[7 + tm:8 + tm, :]
    for ref, val in zip((r_ref, lw_ref, k_ref, v_ref, kap_ref, beta_ref, g_ref), outs):
        ref[...] = val


def _rw_prep_sample_kernel(u_ref, prev_ref, mu_ref, w0_ref, wup_ref, a0_ref, aup_ref, gup_ref, kk_ref, ka_ref,
                           r_ref, lw_ref, k_ref, v_ref, kap_ref, beta_ref, g_ref):
    outs = _rw_prep_body(u_ref[...], prev_ref[...], mu_ref[...], w0_ref[...], wup_ref[...], a0_ref[...],
                         aup_ref[...], gup_ref[...], kk_ref[...], ka_ref[...])
    for ref, val in zip((r_ref, lw_ref, k_ref, v_ref, kap_ref, beta_ref, g_ref), outs):
        ref[...] = val


def _rw_param_specs(nargs):
    full = lambda shape: pl.BlockSpec(shape, lambda *a: (0,) * len(shape))
    del nargs
    return [full((1, RWKV_IN)), full((1, RWKV_WIDTH)), full((64, RWKV_WIDTH)), full((1, RWKV_WIDTH)),
            full((64, RWKV_WIDTH)), full((128, RWKV_WIDTH)), full((1, RWKV_WIDTH)), full((1, RWKV_WIDTH))]


def _rw_prep_prompt(u_rw, nrows, seqlen, params, tm=256):
    outs = [jax.ShapeDtypeStruct((nrows, RWKV_WIDTH), F32)] * 7
    return pl.pallas_call(
        functools.partial(_rw_prep_prompt_kernel, seqlen // tm),
        grid=(nrows // tm,),
        in_specs=[pl.BlockSpec((tm, RWKV_IN), lambda i: (i, 0))] + _rw_param_specs(1),
        out_specs=[pl.BlockSpec((tm, RWKV_WIDTH), lambda i: (i, 0))] * 7,
        out_shape=outs,
        scratch_shapes=[pltpu.VMEM((8 + tm, RWKV_IN), F32)],
        compiler_params=_cp("arbitrary"),
        name="rwkv_prep_prompt",
    )(u_rw, *params)


def _rw_prep_sample(u_rw, row0, dbsz, prev, params):
    outs = [jax.ShapeDtypeStruct((dbsz, RWKV_WIDTH), F32)] * 7
    return pl.pallas_call(
        _rw_prep_sample_kernel,
        grid=(1,),
        in_specs=[pl.BlockSpec((dbsz, RWKV_IN), lambda i: (row0 // dbsz, 0)),
                  pl.BlockSpec((dbsz, RWKV_IN), lambda i: (0, 0))] + _rw_param_specs(1),
        out_specs=[pl.BlockSpec((dbsz, RWKV_WIDTH), lambda i: (0, 0))] * 7,
        out_shape=outs,
        compiler_params=_cp("arbitrary"),
        name="rwkv_prep_sample",
    )(u_rw, prev, *params)


def _rw_post(y, r, k, v, g, rk, ln_g, ln_b):
    mu = _seg_sum(y, RWKV_WIDTH, HEAD_DIM) * (1.0 / HEAD_DIM)
    yc = y - mu
    var = _seg_sum(yc * yc, RWKV_WIDTH, HEAD_DIM) * (1.0 / HEAD_DIM)
    yn = yc * lax.rsqrt(var + RWKV_GN_EPS) * ln_g + ln_b
    bonus = _seg_sum(r * k * rk, RWKV_WIDTH, HEAD_DIM) * v
    return (yn + bonus) * g


def _unit_lower_inverse(a, c, sub):
    row = _iota((c, c), 0)
    col = _iota((c, c), 1)
    eye = jnp.where(row == col, 1.0, 0.0)
    blk = (row // sub) == (col // sub)
    d = [jnp.where(blk, m, 0.0) for m in a]
    n = [m - dm for m, dm in zip(a, d)]
    x = [eye - dm for dm in d]
    p = d
    for _ in range(int(math.log2(sub)) - 1):
        p = [_dot3(m, m, _NN) for m in p]
        x = [xm + _dot3(xm, pm, _NN) for xm, pm in zip(x, p)]
    e = [_dot3(xm, nm, _NN) for xm, nm in zip(x, n)]
    y = [eye - em for em in e]
    p = e
    for _ in range(int(math.log2(c // sub)) - 1):
        p = [_dot3(m, m, _NN) for m in p]
        y = [ym + _dot3(ym, pm, _NN) for ym, pm in zip(y, p)]
    return [_dot3(ym, xm, _NN) for ym, xm in zip(y, x)]


def _rw_scan_kernel(r_ref, lw_ref, k_ref, v_ref, kap_ref, beta_ref, g_ref, rk_ref, lng_ref, lnb_ref,
                    y_ref, sout_ref, s_scr, y_scr):
    c = RWKV_CHUNK
    ci = pl.program_id(1)

    @pl.when(ci == 0)
    def _():
        s_scr[...] = jnp.zeros_like(s_scr)

    r, lw, k, v = r_ref[...], lw_ref[...], k_ref[...], v_ref[...]
    tri = jnp.where(_iota((c, c), 0) >= _iota((c, c), 1), 1.0, 0.0)
    logp = _dot_hi(tri, lw)
    p_inv = jnp.exp(-logp)
    p_end = jnp.exp(logp[c - 1:c, :])
    kr = jnp.concatenate([kap_ref[...] * jnp.exp(logp - lw), r * jnp.exp(logp)], axis=0)
    kb = k * p_inv
    bb = beta_ref[...] * p_inv
    hat = jnp.concatenate([kb * p_end, bb * p_end], axis=0)
    row2 = _iota((2 * c, c), 0)
    col2 = _iota((2 * c, c), 1)
    keep = col2 < jnp.where(row2 < c, row2, row2 - c + 1)
    heads = range(RWKV_HEADS)
    hsl = [slice(h * HEAD_DIM, (h + 1) * HEAD_DIM) for h in heads]
    kr_h = [kr[:, s] for s in hsl]
    v_h = [v[:, s] for s in hsl]
    s0 = [s_scr[h] for h in heads]
    g_b = [jnp.where(keep, _dot3(kr_h[h], bb[:, hsl[h]], _NT), 0.0) for h in heads]
    g_k = [jnp.where(keep, _dot3(kr_h[h], kb[:, hsl[h]], _NT), 0.0) for h in heads]
    tinv = _unit_lower_inverse([m[:c] for m in g_b], c, RWKV_SUB)
    z = [_dot3(kr_h[h], s0[h], _NT) + _dot3(g_k[h], v_h[h], _NN) for h in heads]
    u_m = [_dot3(tinv[h], z[h][:c], _NN) for h in heads]
    for h in heads:
        y_scr[:, hsl[h]] = z[h][c:] - _dot3(g_b[h][c:], u_m[h], _NN)
    for h in heads:
        vu = jnp.concatenate([v_h[h], -u_m[h]], axis=0)
        s_scr[h] = s0[h] * p_end[:, hsl[h]] + _dot3(vu, hat[:, hsl[h]], _TN)

    y_ref[...] = _rw_post(y_scr[...], r, k, v, g_ref[...], rk_ref[...], lng_ref[...], lnb_ref[...]).astype(y_ref.dtype)

    @pl.when(ci == pl.num_programs(1) - 1)
    def _():
        sout_ref[...] = s_scr[...]


def _rw_scan_prompt(seqs, bsz, seqlen, rk, ln_g, ln_b):
    c = RWKV_CHUNK
    nc = seqlen // c
    tok = pl.BlockSpec((c, RWKV_WIDTH), lambda b, i: (b * nc + i, 0))
    par = pl.BlockSpec((1, RWKV_WIDTH), lambda b, i: (0, 0))
    return pl.pallas_call(
        _rw_scan_kernel,
        grid=(bsz, nc),
        in_specs=[tok] * 7 + [par] * 3,
        out_specs=[tok, pl.BlockSpec((None, RWKV_HEADS, HEAD_DIM, HEAD_DIM), lambda b, i: (b, 0, 0, 0))],
        out_shape=[jax.ShapeDtypeStruct((bsz * seqlen, RWKV_WIDTH), BF16),
                   jax.ShapeDtypeStruct((bsz, RWKV_HEADS, HEAD_DIM, HEAD_DIM), F32)],
        scratch_shapes=[pltpu.VMEM((RWKV_HEADS, HEAD_DIM, HEAD_DIM), F32), pltpu.VMEM((c, RWKV_WIDTH), F32)],
        compiler_params=_cp("parallel", "arbitrary"),
        name="rwkv_scan_prompt",
    )(*seqs, rk, ln_g, ln_b)


def _rw_step_kernel(r_ref, lw_ref, k_ref, v_ref, kap_ref, beta_ref, g_ref, s0_ref, rk_ref, lng_ref, lnb_ref,
                    y_ref, sout_ref):
    bt = r_ref.shape[0]
    rows = _iota((bt, 1), 0)
    r, k, v = r_ref[...], k_ref[...], v_ref[...]
    dec = jnp.exp(lw_ref[...])
    kap, beta = kap_ref[...], beta_ref[...]
    seqs = range(bt)
    r_b = [jnp.where(rows == bi, r, 0.0) for bi in seqs]
    v_b = [jnp.where(rows == bi, v, 0.0) for bi in seqs]
    y_cols = []
    for h in range(RWKV_HEADS):
        hs = slice(h * HEAD_DIM, (h + 1) * HEAD_DIM)
        s0 = [s0_ref[bi, h] for bi in seqs]
        sa = [jnp.sum(s0[bi] * (-kap[bi:bi + 1, hs]), -1, keepdims=True) for bi in seqs]
        vk = [_dot3(v_b[bi][:, hs], k[:, hs], _TN) for bi in seqs]
        sn = [s0[bi] * dec[bi:bi + 1, hs] + sa[bi] * beta[bi:bi + 1, hs] + vk[bi] for bi in seqs]
        for bi in seqs:
            sout_ref[bi, h] = sn[bi]
        ys = [_dot3(r_b[bi][:, hs], sn[bi], _NT) for bi in seqs]
        y_cols.append(functools.reduce(lambda a, b: a + b, ys))
    y_all = jnp.concatenate(y_cols, axis=-1)
    y_ref[...] = _rw_post(y_all, r, k, v, g_ref[...], rk_ref[...], lng_ref[...], lnb_ref[...]).astype(y_ref.dtype)


def _rw_step_sample(seqs, s0, dbsz, rk, ln_g, ln_b, bt=8):
    tok = pl.BlockSpec((bt, RWKV_WIDTH), lambda i: (i, 0))
    par = pl.BlockSpec((1, RWKV_WIDTH), lambda i: (0, 0))
    st = pl.BlockSpec((bt, RWKV_HEADS, HEAD_DIM, HEAD_DIM), lambda i: (i, 0, 0, 0))
    return pl.pallas_call(
        _rw_step_kernel,
        grid=(dbsz // bt,),
        in_specs=[tok] * 7 + [st] + [par] * 3,
        out_specs=[tok, st],
        out_shape=[jax.ShapeDtypeStruct((dbsz, RWKV_WIDTH), F32),
                   jax.ShapeDtypeStruct((dbsz, RWKV_HEADS, HEAD_DIM, HEAD_DIM), F32)],
        compiler_params=_cp("parallel"),
        name="rwkv_step_sample",
    )(*seqs, s0, rk, ln_g, ln_b)


def _mla_proj_kernel(u_ref, cos_ref, sin_ref, qn_ref, kvn_ref, wqn_ref, wqr_ref, wqx_ref, wkn_ref,
                     ckv_ref, kpe_ref, q_ref, kcat_ref):
    tm = u_ref.shape[0]
    c_q = _rms_norm(u_ref[:, :MLA_Q_RANK], qn_ref[...])
    c_kv = _rms_norm(u_ref[:, MLA_Q_RANK:MLA_Q_RANK + MLA_KV_RANK], kvn_ref[...])
    cos = cos_ref[...]
    sin = sin_ref[...]
    o = MLA_Q_RANK + MLA_KV_RANK
    k_pe = u_ref[:, o:o + MLA_ROPE] * cos[:, :MLA_ROPE] + u_ref[:, o + MLA_ROPE:o + 2 * MLA_ROPE] * sin[:, :MLA_ROPE]
    ckv_ref[...] = c_kv
    kpe_ref[...] = k_pe
    zpad = jnp.zeros((tm, MLA_QK - MLA_KV_RANK - MLA_ROPE), F32)
    kcat_ref[...] = jnp.concatenate([c_kv, k_pe, zpad], axis=-1).astype(BF16)
    q_nope = _wdot(c_q, wqn_ref[...])
    q_pe = _wdot(c_q, wqr_ref[...]) * cos + _wdot(c_q, wqx_ref[...]) * sin
    for h in range(MLA_HEADS):
        q_lat = _wdot(q_nope[:, h * MLA_NOPE:(h + 1) * MLA_NOPE], wkn_ref[h])
        qh = jnp.concatenate([q_lat, q_pe[:, h * MLA_ROPE:(h + 1) * MLA_ROPE], zpad], axis=-1) * MLA_SCALE
        q_ref[h] = qh.astype(q_ref.dtype)


def _mla_proj(u_mla, cos5, sin5, qn, kvn, wqn, wqr, wqx, wkn, tm):
    t = u_mla.shape[0]
    tm = _row_tile(t, tm)
    full = lambda shape: pl.BlockSpec(shape, lambda i: (0,) * len(shape))
    hr = MLA_HEADS * MLA_ROPE
    return pl.pallas_call(
        _mla_proj_kernel,
        grid=(t // tm,),
        in_specs=[pl.BlockSpec((tm, MLA_UP), lambda i: (i, 0)),
                  pl.BlockSpec((tm, hr), lambda i: (i, 0)), pl.BlockSpec((tm, hr), lambda i: (i, 0)),
                  full((1, MLA_Q_RANK)), full((1, MLA_KV_RANK)),
                  full((MLA_Q_RANK, MLA_HEADS * MLA_NOPE)), full((MLA_Q_RANK, hr)), full((MLA_Q_RANK, hr)),
                  full((MLA_HEADS, MLA_NOPE, MLA_KV_RANK))],
        out_specs=[pl.BlockSpec((tm, MLA_KV_RANK), lambda i: (i, 0)),
                   pl.BlockSpec((tm, MLA_ROPE), lambda i: (i, 0)),
                   pl.BlockSpec((MLA_HEADS, tm, MLA_QK), lambda i: (0, i, 0)),
                   pl.BlockSpec((tm, MLA_QK), lambda i: (i, 0))],
        out_shape=[jax.ShapeDtypeStruct((t, MLA_KV_RANK), F32), jax.ShapeDtypeStruct((t, MLA_ROPE), F32),
                   jax.ShapeDtypeStruct((MLA_HEADS, t, MLA_QK), wqn.dtype), jax.ShapeDtypeStruct((t, MLA_QK), BF16)],
        compiler_params=_cp("parallel"),
        name="mla_proj",
    )(u_mla, cos5, sin5, qn, kvn, wqn, wqr, wqx, wkn)


def _mla_prompt_kernel(tq, tk, q_ref, k_ref, wv_ref, y_ref, m_scr, l_scr, acc_scr):
    i = pl.program_id(1)
    j = pl.program_id(2)
    rows = MLA_HEADS * tq
    last = (i * tq + tq - 1) // tk

    @pl.when(j == 0)
    def _():
        m_scr[...] = jnp.full(m_scr.shape, NEG_BIG, F32)
        l_scr[...] = jnp.zeros_like(l_scr)
        acc_scr[...] = jnp.zeros_like(acc_scr)

    @pl.when(j <= last)
    def _():
        q = q_ref[...].reshape(rows, MLA_QK)
        kc = k_ref[...]
        s = lax.dot_general(q, kc, (((1,), (1,)), ((), ())), preferred_element_type=F32)
        qpos = i * tq + _iota((rows, tk), 0) % tq
        kpos = j * tk + _iota((rows, tk), 1)
        s = jnp.where(kpos <= qpos, s, NEG_BIG)
        m_old = m_scr[...]
        m_new = jnp.maximum(m_old, jnp.max(s, -1, keepdims=True))
        alpha = jnp.exp(m_old - m_new)
        p = jnp.exp(s - m_new)
        l_scr[...] = alpha * l_scr[...] + jnp.sum(p, -1, keepdims=True)
        acc_scr[...] = alpha * acc_scr[...] + jnp.dot(p.astype(BF16), kc[:, :MLA_KV_RANK],
                                                      preferred_element_type=F32)
        m_scr[...] = m_new

    @pl.when(j == pl.num_programs(2) - 1)
    def _():
        o = (acc_scr[...] / l_scr[...]).reshape(MLA_HEADS, tq, MLA_KV_RANK)
        ys = [_dot(o[h], wv_ref[h]) for h in range(MLA_HEADS)]
        y_ref[...] = jnp.concatenate(ys, axis=-1).astype(y_ref.dtype)


def _mla_prompt_attend(q, kcat, wv, bsz, seqlen, tq=256, tk=512):
    nq = seqlen // tq
    nk = seqlen // tk
    rows = MLA_HEADS * tq

    def k_map(b, i, j):
        return (b * nk + jnp.minimum(j, (i * tq + tq - 1) // tk), 0)

    return pl.pallas_call(
        functools.partial(_mla_prompt_kernel, tq, tk),
        grid=(bsz, nq, nk),
        in_specs=[pl.BlockSpec((MLA_HEADS, tq, MLA_QK), lambda b, i, j: (0, b * nq + i, 0)),
                  pl.BlockSpec((tk, MLA_QK), k_map),
                  pl.BlockSpec((MLA_HEADS, MLA_KV_RANK, MLA_V), lambda b, i, j: (0, 0, 0))],
        out_specs=pl.BlockSpec((tq, MLA_WIDTH), lambda b, i, j: (b * nq + i, 0)),
        out_shape=jax.ShapeDtypeStruct((bsz * seqlen, MLA_WIDTH), BF16),
        scratch_shapes=[pltpu.VMEM((rows, 1), F32), pltpu.VMEM((rows, 1), F32),
                        pltpu.VMEM((rows, MLA_KV_RANK), F32)],
        compiler_params=_cp("parallel", "parallel", "arbitrary"),
        name="mla_prompt_attend",
    )(q, kcat, wv)


def _mla_sample_kernel(npg, pt_ref, q_ref, cnew_ref, pnew_ref, *refs):
    del pt_ref
    ckv_refs = refs[:npg]
    kpe_refs = refs[npg:2 * npg]
    o_ref = refs[2 * npg]
    m_scr, l_scr, acc_scr = refs[2 * npg + 1:]
    s_idx = pl.program_id(1)

    @pl.when(s_idx == 0)
    def _():
        m_scr[...] = jnp.full(m_scr.shape, NEG_BIG, F32)
        l_scr[...] = jnp.zeros_like(l_scr)
        acc_scr[...] = jnp.zeros_like(acc_scr)

    hp = q_ref.shape[0]

    def split(x):
        hi = x.astype(BF16)
        return jnp.concatenate([hi, (x - hi.astype(F32)).astype(BF16)], axis=0)

    q = q_ref[...]
    q2 = split(q)
    q_lat = q2[:, :MLA_KV_RANK]
    q_pe = q2[:, MLA_KV_RANK:MLA_KV_RANK + MLA_ROPE]
    cs = [r[...].astype(BF16) for r in ckv_refs]
    ss = [lax.dot_general(q_lat, c, _NT, preferred_element_type=F32)
          + jnp.dot(q_pe, r[...].astype(BF16), preferred_element_type=F32)
          for c, r in zip(cs, kpe_refs)]
    s2 = jnp.concatenate(ss, axis=-1)
    s = s2[:hp] + s2[hp:]
    m_old = m_scr[...]
    m_new = jnp.maximum(m_old, jnp.max(s, -1, keepdims=True))
    alpha = jnp.exp(m_old - m_new)
    p = jnp.exp(s - m_new)
    l_scr[...] = alpha * l_scr[...] + jnp.sum(p, -1, keepdims=True)
    p2 = split(p)
    pv = [jnp.dot(p2[:, n * PAGE_SIZE:(n + 1) * PAGE_SIZE], c, preferred_element_type=F32)
          for n, c in enumerate(cs)]
    pv = functools.reduce(lambda a, b: a + b, pv)
    acc_scr[...] = alpha * acc_scr[...] + pv[:hp] + pv[hp:]
    m_scr[...] = m_new

    @pl.when(s_idx == pl.num_programs(1) - 1)
    def _():
        cn = cnew_ref[...]
        pn = pnew_ref[...]
        s_new = (jnp.sum(q[:, :MLA_KV_RANK] * cn, -1, keepdims=True)
                 + jnp.sum(q[:, MLA_KV_RANK:MLA_KV_RANK + MLA_ROPE] * pn, -1, keepdims=True))
        m_fin = jnp.maximum(m_scr[...], s_new)
        a_fin = jnp.exp(m_scr[...] - m_fin)
        p_new = jnp.exp(s_new - m_fin)
        l_fin = a_fin * l_scr[...] + p_new
        o_ref[...] = (a_fin * acc_scr[...] + p_new * cn) / l_fin


def _mla_sample_attend(layer, q_s, c_new, p_new, cache_ckv, cache_kpe_t, page_table, npg=32):
    dbsz, n_pages = page_table.shape
    hp = q_s.shape[1]

    def page_map(n):
        return lambda b, s, pt: (layer, pt[b, s * npg + n], 0, 0)

    grid_spec = pltpu.PrefetchScalarGridSpec(
        num_scalar_prefetch=1,
        grid=(dbsz, n_pages // npg),
        in_specs=[pl.BlockSpec((None, hp, MLA_QK), lambda b, s, pt: (b, 0, 0)),
                  pl.BlockSpec((None, 1, MLA_KV_RANK), lambda b, s, pt: (b, 0, 0)),
                  pl.BlockSpec((None, 1, MLA_ROPE), lambda b, s, pt: (b, 0, 0))]
                 + [pl.BlockSpec((None, None, PAGE_SIZE, MLA_KV_RANK), page_map(n)) for n in range(npg)]
                 + [pl.BlockSpec((None, None, MLA_ROPE, PAGE_SIZE), page_map(n)) for n in range(npg)],
        out_specs=pl.BlockSpec((None, hp, MLA_KV_RANK), lambda b, s, pt: (b, 0, 0)),
        scratch_shapes=[pltpu.VMEM((hp, 1), F32), pltpu.VMEM((hp, 1), F32), pltpu.VMEM((hp, MLA_KV_RANK), F32)],
    )
    return pl.pallas_call(
        functools.partial(_mla_sample_kernel, npg),
        grid_spec=grid_spec,
        out_shape=jax.ShapeDtypeStruct((dbsz, hp, MLA_KV_RANK), F32),
        compiler_params=_cp("parallel", "arbitrary"),
        name="mla_sample_attend",
    )(page_table, q_s, c_new, p_new, *([cache_ckv] * npg), *([cache_kpe_t] * npg))


def _mla_vup_kernel(o_ref, wv_ref, y_ref):
    ys = [_wdot(o_ref[h], wv_ref[h]) for h in range(MLA_HEADS)]
    y_ref[...] = jnp.concatenate(ys, axis=-1).astype(y_ref.dtype)


def _mla_vup(o_heads, wv):
    hp, dbsz, _ = o_heads.shape
    return pl.pallas_call(
        _mla_vup_kernel,
        grid=(1,),
        in_specs=[pl.BlockSpec((hp, dbsz, MLA_KV_RANK), lambda i: (0, 0, 0)),
                  pl.BlockSpec((MLA_HEADS, MLA_KV_RANK, MLA_V), lambda i: (0, 0, 0))],
        out_specs=pl.BlockSpec((dbsz, MLA_WIDTH), lambda i: (0, 0)),
        out_shape=jax.ShapeDtypeStruct((dbsz, MLA_WIDTH), F32),
        compiler_params=_cp("arbitrary"),
        name="mla_vup",
    )(o_heads, wv)


def _rope_tables(pos):
    half = MLA_ROPE // 2
    inv = ROPE_THETA ** (-jnp.arange(half, dtype=F32) / half)
    ang = pos.astype(F32)[:, None] * inv[None, :]
    cos = jnp.tile(jnp.cos(ang), (1, 2 * MLA_HEADS))
    sin = jnp.tile(jnp.sin(ang), (1, 2 * MLA_HEADS))
    return cos, sin


def _pad_lanes(v, width=LANES):
    return jnp.pad(v, (0, width - v.shape[0])).reshape(1, width)


def _rotate_half_cols(w):
    half = MLA_ROPE // 2
    return jnp.concatenate([-w[..., half:], w[..., :half]], axis=-1)


def kernel(x_prompt, x_sample, cache_ckv, cache_kpe, page_table, state_ssm, state_conv, state_wkv, state_shift,
           w_in, w_out, ln1_g, ln1_b, ln2_g, ln2_b,
           ssd_conv_w, ssd_conv_b, ssd_dt_bias, ssd_a_log, ssd_d, ssd_norm,
           mla_q_norm, mla_kv_norm, mla_q_up, mla_kv_up,
           rw_mu, rw_w0, rw_w_up, rw_a0, rw_a_up, rw_g_up, rw_k_k, rw_k_a, rw_r_k, rw_ln_g, rw_ln_b,
           ffn_gu, ffn_down, moe_router, moe_gu, moe_down):
    bsz, seqlen, d = x_prompt.shape
    dbsz = x_sample.shape[0]
    n_p = bsz * seqlen
    past_len = page_table.shape[1] * PAGE_SIZE
    xp = x_prompt.reshape(n_p, d)
    xs = x_sample.reshape(dbsz, d)
    tm = 512
    cos_p, sin_p = _rope_tables(jnp.tile(jnp.arange(seqlen), bsz))
    cos_s, sin_s = _rope_tables(jnp.full((dbsz,), past_len))
    cache_kpe_t = jnp.swapaxes(cache_kpe, 2, 3)
    row = lambda v: v.reshape(1, -1)
    lo = lambda w: w.astype(BF16)

    outs_p = [[] for _ in range(6)]
    outs_s = [[] for _ in range(6)]
    for l in range(DEPTH):
        wi = w_in[l]
        m0 = SSD_IN
        r0 = SSD_IN + MLA_IN
        w_ssd = jnp.pad(wi[:, :SSD_IN], ((0, 0), (0, SSD_UP - SSD_IN)))
        k_rope_w = wi[:, m0 + 1024:m0 + MLA_IN]
        w_mla = jnp.concatenate([wi[:, m0:m0 + MLA_IN], _rotate_half_cols(k_rope_w)], axis=1)
        w_rw = wi[:, r0:]
        q_up = mla_q_up[l]
        wqn = q_up[:, :, :MLA_NOPE].reshape(MLA_Q_RANK, -1)
        wqr = q_up[:, :, MLA_NOPE:].reshape(MLA_Q_RANK, -1)
        wqx = _rotate_half_cols(q_up[:, :, MLA_NOPE:]).reshape(MLA_Q_RANK, -1)
        kv_up = mla_kv_up[l]
        wkn = jnp.transpose(kv_up[:, :, :MLA_NOPE], (1, 2, 0))
        wv = jnp.transpose(kv_up[:, :, MLA_NOPE:], (1, 0, 2))
        mla_w = (wqn, wqr, wqx, wkn)
        mla_n = (row(mla_q_norm[l]), row(mla_kv_norm[l]))
        ssd_par = (ssd_conv_w[l], row(ssd_conv_b[l]), _pad_lanes(ssd_dt_bias[l]), _pad_lanes(ssd_a_log[l]))
        d_rep = row(jnp.repeat(ssd_d[l], HEAD_DIM))
        a_rep = row(jnp.repeat(-jnp.exp(ssd_a_log[l]), HEAD_DIM))
        rw_par_s = (row(rw_mu[l]), row(rw_w0[l]), rw_w_up[l], row(rw_a0[l]), rw_a_up[l], rw_g_up[l],
                    row(rw_k_k[l]), row(rw_k_a[l]))
        rw_par_p = tuple(lo(w) if w.shape[0] > 1 else w for w in rw_par_s)
        rw_post = (row(rw_r_k[l]), row(rw_ln_g[l]), row(rw_ln_b[l]))
        ln1 = (row(ln1_g[l]), row(ln1_b[l]))
        ln2 = (row(ln2_g[l]), row(ln2_b[l]))

        u_ssd = _matmul(xp, lo(w_ssd), tm)
        u_mla = _matmul(xp, lo(w_mla), tm)
        u_rw = _matmul(xp, lo(w_rw), tm)
        y_ssd_p, ssm_p = _ssd_prompt(u_ssd, bsz, seqlen, *ssd_par, d_rep, row(ssd_norm[l]))
        conv_p = u_ssd[:, SSD_WIDTH:SSD_WIDTH + SSD_XBC].reshape(bsz, seqlen, SSD_XBC)[:, seqlen - (SSD_CONV - 1):]
        ckv_p, kpe_p, q_p, kcat = _mla_proj(u_mla, cos_p, sin_p, *mla_n, *[lo(w) for w in mla_w], tm)
        y_mla_p = _mla_prompt_attend(q_p, kcat, lo(wv), bsz, seqlen)
        seq_p = _rw_prep_prompt(u_rw, n_p, seqlen, rw_par_p)
        y_rw_p, wkv_p = _rw_scan_prompt(seq_p, bsz, seqlen, *rw_post)
        shift_p = u_rw.reshape(bsz, seqlen, RWKV_IN)[:, -1]

        us_ssd = _matmul(xs, w_ssd, dbsz)
        us_mla = _matmul(xs, w_mla, dbsz)
        us_rw = _matmul(xs, w_rw, dbsz)
        y_ssd_s, ssm_s = _ssd_sample(l, us_ssd, 0, dbsz, state_conv[l].reshape(dbsz, -1), state_ssm,
                                     *ssd_par, a_rep, d_rep, row(ssd_norm[l]))
        conv_s = jnp.concatenate([state_conv[l][:, 1:], us_ssd[:, None, SSD_WIDTH:SSD_WIDTH + SSD_XBC]], axis=1)
        ckv_s, kpe_s, q_s, _ = _mla_proj(us_mla, cos_s, sin_s, *mla_n, *mla_w, dbsz)
        q_s = jnp.pad(jnp.transpose(q_s, (1, 0, 2)), ((0, 0), (0, 8 - MLA_HEADS), (0, 0)))
        o_lat = _mla_sample_attend(l, q_s, ckv_s[:, None, :], kpe_s[:, None, :], cache_ckv, cache_kpe_t, page_table)
        y_mla_s = _mla_vup(jnp.transpose(o_lat, (1, 0, 2)), wv)
        seq_s = _rw_prep_sample(us_rw, 0, dbsz, state_shift[l], rw_par_s)
        y_rw_s, wkv_s = _rw_step_sample(seq_s, state_wkv[l], dbsz, *rw_post)

        xp = _matmul_res_ln(jnp.concatenate([y_ssd_p, y_mla_p, y_rw_p], axis=-1), lo(w_out[l]), xp, *ln1, tm)
        xs = _matmul_res_ln(jnp.concatenate([y_ssd_s, y_mla_s, y_rw_s], axis=-1), w_out[l], xs, *ln1, dbsz)
        if l % 2 == 1:
            wr = jnp.pad(moe_router[l // 2], ((0, 0), (0, LANES - N_EXPERTS)))
            w_dn = lo(moe_down[l // 2])
            w_gu = jnp.transpose(lo(moe_gu[l // 2]).reshape(N_EXPERTS, d, 2, FFN_EXPERT // 256, 256), (0, 2, 3, 1, 4))
            xp = _moe_prompt(xp, _router(xp, wr, tm), w_gu, w_dn, *ln2, tm, 256, MOE_CAP)
            xs = _moe_dense(xs, _router(xs, wr, dbsz), w_gu, w_dn, *ln2, dbsz, 256)
        else:
            xp = _ffn_dense(xp, lo(ffn_gu[l // 2]), lo(ffn_down[l // 2]), *ln2, tm, 512)
            xs = _ffn_dense(xs, ffn_gu[l // 2], ffn_down[l // 2], *ln2, dbsz, 512)

        for lst, val in zip(outs_p, (ckv_p.reshape(bsz, seqlen, -1), kpe_p.reshape(bsz, seqlen, -1),
                                     ssm_p, conv_p, wkv_p, shift_p)):
            lst.append(val)
        for lst, val in zip(outs_s, (ckv_s.reshape(dbsz, 1, -1), kpe_s.reshape(dbsz, 1, -1),
                                     ssm_s, conv_s, wkv_s, us_rw)):
            lst.append(val)

    ckv_p, kpe_p, ssm_p, conv_p, wkv_p, shift_p = [jnp.stack(o) for o in outs_p]
    ckv_s, kpe_s, ssm_s, conv_s, wkv_s, shift_s = [jnp.stack(o) for o in outs_s]
    return (xp.reshape(bsz, seqlen, d), xs.reshape(dbsz, 1, d), ckv_p, kpe_p, ckv_s, kpe_s, ssm_p, ssm_s,
            conv_p, conv_s, wkv_p, wkv_s, shift_p, shift_s)
```
